```python
import math
import jax
import jax.numpy as jnp
from jax import lax
import numpy as np

D_MODEL = 4096
BATCH = 4
SEQ = 4096
DEPTH = 2

CHUNK = 64
HEAD_DIM = 128
D_MIX = D_MODEL
N_MIXERS = 4
HEADS_PER_GROUP = D_MIX // (N_MIXERS * HEAD_DIM)
D_GROUP = HEADS_PER_GROUP * HEAD_DIM
SGU_BLOCK = 128
Q_BLOCK = 128
CONV_WIDTH = 3
DIFF_DIM = HEAD_DIM // 2
D_FF = ((8 * D_MODEL // 3 + 255) // 256) * 256
N_EXPERTS = 8
TOP_K = 2
D_EXPERT = D_MODEL
N_DENSE = (DEPTH + 1) // 2
N_MOE = DEPTH // 2
RMS_EPS = 1e-6
NEG_INF = -1e30
SPLIT_WIDTHS = (2 * D_GROUP, D_GROUP, D_GROUP, D_GROUP, HEADS_PER_GROUP,
                D_GROUP, D_GROUP, D_GROUP, D_GROUP, D_GROUP, D_GROUP)
N_IN = 11 * D_GROUP + HEADS_PER_GROUP

kernel_name = "hybrid_chunk_causal_head_group_trunk"


def rmsnorm(x, g):
    x32 = x.astype(jnp.float32)
    y = x32 * lax.rsqrt(jnp.mean(x32 * x32, axis=-1, keepdims=True) + RMS_EPS)
    return (y * g.astype(jnp.float32)).astype(x.dtype)


def layernorm_plain(x):
    x32 = x.astype(jnp.float32)
    mu = jnp.mean(x32, axis=-1, keepdims=True)
    var = jnp.mean(jnp.square(x32 - mu), axis=-1, keepdims=True)
    return ((x32 - mu) * lax.rsqrt(var + RMS_EPS)).astype(x.dtype)


def sgu_mixer(z, w_s, b_s):
    bsz, s_len, _ = z.shape
    z = jax.nn.gelu(z)
    u, v = jnp.split(z, 2, axis=-1)
    v = layernorm_plain(v)
    nb = s_len // SGU_BLOCK
    v = v.reshape(bsz, nb, SGU_BLOCK, HEADS_PER_GROUP, HEAD_DIM)
    chunk_id = jnp.arange(SGU_BLOCK) // CHUNK
    mask = (chunk_id[:, None] >= chunk_id[None, :]).astype(w_s.dtype)
    w = w_s * mask[None]
    mixed = jnp.einsum('hts,bnshd->bnthd', w, v) + b_s.T[:, :, None]
    return u * mixed.reshape(bsz, s_len, D_GROUP)


def fox_attention(q, k, v, f_logit):
    bsz, s_len, nh, dh = q.shape
    nb = s_len // Q_BLOCK
    c = jnp.cumsum(jax.nn.log_sigmoid(f_logit.astype(jnp.float32)), axis=1)
    c = c.transpose(0, 2, 1)
    q_blocks = q.reshape(bsz, nb, Q_BLOCK, nh, dh).transpose(1, 0, 3, 2, 4)
    c_blocks = c.reshape(bsz, nh, nb, Q_BLOCK).transpose(2, 0, 1, 3)
    kpos = jnp.arange(s_len)
    scale = dh ** -0.5

    def one_block(args):
        i, q_i, c_i = args
        s = jnp.einsum('bhqd,bkhd->bhqk', q_i, k).astype(jnp.float32) * scale
        s = s + c_i[..., None] - c[:, :, None, :]
        qpos = i * Q_BLOCK + jnp.arange(Q_BLOCK)
        s = jnp.where(kpos[None, :] <= qpos[:, None], s, NEG_INF)
        p = jax.nn.softmax(s, axis=-1)
        return jnp.einsum('bhqk,bkhd->bqhd', p.astype(v.dtype), v)

    out = lax.map(one_block, (jnp.arange(nb), q_blocks, c_blocks))
    return out.transpose(1, 0, 2, 3, 4).reshape(bsz, s_len, nh * dh)


def short_conv_mixer(x_in, b_gate, c_gate, conv_w):
    s_len = x_in.shape[1]
    h = c_gate * x_in
    hp = jnp.pad(h, ((0, 0), (CONV_WIDTH - 1, 0), (0, 0)))
    y = conv_w[0] * hp[:, 0:s_len]
    for tap in range(1, CONV_WIDTH):
        y = y + conv_w[tap] * hp[:, tap:tap + s_len]
    return b_gate * y


def diff_attention(q, k, v, lam, lam_init, subln_g):
    bsz, s_len, nh, dh = v.shape
    nb = s_len // Q_BLOCK
    k = k.reshape(bsz, s_len, nh, 2, DIFF_DIM)
    q_blocks = q.reshape(bsz, nb, Q_BLOCK, nh, 2, DIFF_DIM).transpose(1, 0, 2, 3, 4, 5)
    slopes = 2.0 ** (-8.0 * jnp.arange(1, nh + 1, dtype=jnp.float32) / nh)
    kpos = jnp.arange(s_len)
    scale = DIFF_DIM ** -0.5

    def one_block(args):
        i, q_i = args
        s = jnp.einsum('bqhmd,bkhmd->bmhqk', q_i, k).astype(jnp.float32) * scale
        qpos = i * Q_BLOCK + jnp.arange(Q_BLOCK)
        dist = jnp.abs(qpos[:, None] - kpos[None, :]).astype(jnp.float32)
        s = s - slopes[:, None, None] * dist
        allowed = (kpos // CHUNK)[None, :] <= (qpos // CHUNK)[:, None]
        s = jnp.where(allowed, s, NEG_INF)
        p = jax.nn.softmax(s, axis=-1)
        a = p[:, 0] - lam * p[:, 1]
        return jnp.einsum('bhqk,bkhd->bqhd', a.astype(v.dtype), v)

    out = lax.map(one_block, (jnp.arange(nb), q_blocks))
    out = out.transpose(1, 0, 2, 3, 4).reshape(bsz, s_len, nh, dh)
    out = rmsnorm(out, subln_g) * (1.0 - lam_init)
    return out.reshape(bsz, s_len, nh * dh)


def swiglu(h, w1, w3, w2):
    return (jax.nn.silu(h @ w1) * (h @ w3)) @ w2


def moe_swiglu(h, router_w, w1, w3, w2):
    bsz, s_len, d = h.shape
    t = h.reshape(-1, d)
    logits = (t @ router_w).astype(jnp.float32)
    top_v, top_i = lax.top_k(logits, TOP_K)
    gates = jax.nn.softmax(top_v, axis=-1)
    combine = jnp.sum(jax.nn.one_hot(top_i, N_EXPERTS, dtype=jnp.float32) * gates[..., None], axis=1)
    out = jnp.zeros_like(t)
    for e in range(N_EXPERTS):
        out = out + combine[:, e:e + 1].astype(t.dtype) * swiglu(t, w1[e], w3[e], w2[e])
    return out.reshape(bsz, s_len, d)


def setup_inputs(seed: int = 0) -> dict:
    key = jax.random.key(seed)
    ks = jax.random.split(key, 24)
    f32 = jnp.float32

    def nrm(k, shape, scale):
        return jax.random.normal(k, shape, f32) * scale

    return {
        'x': nrm(ks[0], (BATCH, SEQ, D_MODEL), 1.0),
        'ln1_g': 1.0 + nrm(ks[1], (DEPTH, D_MODEL), 0.02),
        'w_in': nrm(ks[2], (DEPTH, D_MODEL, N_IN), D_MODEL ** -0.5),
        'b_f': nrm(ks[3], (DEPTH, HEADS_PER_GROUP), 0.1),
        'w_s': nrm(ks[4], (DEPTH, HEADS_PER_GROUP, SGU_BLOCK, SGU_BLOCK), SGU_BLOCK ** -0.5),
        'b_s': 1.0 + nrm(ks[5], (DEPTH, HEADS_PER_GROUP, SGU_BLOCK), 0.1),
        'conv_w': nrm(ks[6], (DEPTH, CONV_WIDTH, D_GROUP), CONV_WIDTH ** -0.5),
        'lam_q1': nrm(ks[7], (DEPTH, DIFF_DIM), 0.1),
        'lam_k1': nrm(ks[8], (DEPTH, DIFF_DIM), 0.1),
        'lam_q2': nrm(ks[9], (DEPTH, DIFF_DIM), 0.1),
        'lam_k2': nrm(ks[10], (DEPTH, DIFF_DIM), 0.1),
        'subln_g': 1.0 + nrm(ks[11], (DEPTH, HEAD_DIM), 0.02),
        'w_o': nrm(ks[12], (DEPTH, D_MIX, D_MODEL), D_MIX ** -0.5),
        'ln2_g': 1.0 + nrm(ks[13], (DEPTH, D_MODEL), 0.02),
        'ffn_w1': nrm(ks[14], (N_DENSE, D_MODEL, D_FF), D_MODEL ** -0.5),
        'ffn_w3': nrm(ks[15], (N_DENSE, D_MODEL, D_FF), D_MODEL ** -0.5),
        'ffn_w2': nrm(ks[16], (N_DENSE, D_FF, D_MODEL), D_FF ** -0.5),
        'router_w': nrm(ks[17], (N_MOE, D_MODEL, N_EXPERTS), D_MODEL ** -0.5),
        'moe_w1': nrm(ks[18], (N_MOE, N_EXPERTS, D_MODEL, D_EXPERT), D_MODEL ** -0.5),
        'moe_w3': nrm(ks[19], (N_MOE, N_EXPERTS, D_MODEL, D_EXPERT), D_MODEL ** -0.5),
        'moe_w2': nrm(ks[20], (N_MOE, N_EXPERTS, D_EXPERT, D_MODEL), D_EXPERT ** -0.5),
        'final_g': 1.0 + nrm(ks[21], (D_MODEL,), 0.02),
    }


def reference(x, ln1_g, w_in, b_f, w_s, b_s, conv_w, lam_q1, lam_k1, lam_q2, lam_k2,
              subln_g, w_o, ln2_g, ffn_w1, ffn_w3, ffn_w2, router_w, moe_w1, moe_w3,
              moe_w2, final_g):
    bsz, s_len, _ = x.shape
    offsets = np.cumsum(SPLIT_WIDTHS)[:-1].tolist()
    h = x
    for i in range(DEPTH):
        n = rmsnorm(h, ln1_g[i])
        proj = n @ w_in[i]
        (z_sgu, q_f, k_f, v_f, f_f, c_in, c_b, c_c,
         q_d, k_d, v_d) = jnp.split(proj, offsets, axis=-1)

        out_a = sgu_mixer(z_sgu, w_s[i], b_s[i])

        heads = (bsz, s_len, HEADS_PER_GROUP, HEAD_DIM)
        out_b = fox_attention(q_f.reshape(heads), k_f.reshape(heads), v_f.reshape(heads),
                              f_f + b_f[i])

        out_c = short_conv_mixer(c_in, c_b, c_c, conv_w[i])

        lam_init = 0.8 - 0.6 * math.exp(-0.3 * i)
        lam = (jnp.exp(jnp.dot(lam_q1[i].astype(jnp.float32), lam_k1[i].astype(jnp.float32)))
               - jnp.exp(jnp.dot(lam_q2[i].astype(jnp.float32), lam_k2[i].astype(jnp.float32)))
               + lam_init)
        out_d = diff_attention(q_d, k_d, v_d.reshape(heads), lam, lam_init, subln_g[i])

        mix = jnp.concatenate([out_a, out_b, out_c, out_d], axis=-1)
        h = h + mix @ w_o[i]

        n = rmsnorm(h, ln2_g[i])
        if i % 2 == 0:
            j = i // 2
            h = h + swiglu(n, ffn_w1[j], ffn_w3[j], ffn_w2[j])
        else:
            j = i // 2
            h = h + moe_swiglu(n, router_w[j], moe_w1[j], moe_w3[j], moe_w2[j])
    return rmsnorm(h, final_g)
```

```python
import functools
import math

import jax
import jax.numpy as jnp
from jax import lax
from jax.experimental import pallas as pl
from jax.experimental.pallas import tpu as pltpu

F32 = jnp.float32
BF16 = jnp.bfloat16

HEAD_DIM = 128
CHUNK = 64
SGU_BLOCK = 128
CONV_WIDTH = 3
TOP_K = 2
RMS_EPS = 1e-6
NEG_INF = -1e30
LANES = 128
BF16_ROWS = 16
VMEM_LIMIT = 58 * 1024 * 1024


def _cparams(sem, vmem=None):
    return pltpu.CompilerParams(dimension_semantics=sem, vmem_limit_bytes=vmem)


def _tile(dim, target, unit=LANES):
    if dim <= target:
        return dim
    best = max(c for c in range(unit, target + 1, unit) if dim % c == 0)
    return best


def _rms(x, g):
    ms = jnp.mean(x * x, axis=-1, keepdims=True)
    return x * lax.rsqrt(ms + RMS_EPS) * g


def _ln1_body(x_ref, g_ref, wf_ref, n_ref, f_ref):
    nb = _rms(x_ref[...], g_ref[...]).astype(BF16)
    n_ref[...] = nb
    f_ref[...] = jnp.dot(nb, wf_ref[...], preferred_element_type=F32)


def _ln1(h, g, wf, tm=256):
    t, d = h.shape
    tm = min(tm, t)
    return pl.pallas_call(
        _ln1_body,
        out_shape=(jax.ShapeDtypeStruct((t, d), BF16), jax.ShapeDtypeStruct((t, LANES), F32)),
        grid=(t // tm,),
        in_specs=[pl.BlockSpec((tm, d), lambda i: (i, 0)),
                  pl.BlockSpec((1, d), lambda i: (0, 0)),
                  pl.BlockSpec((d, LANES), lambda i: (0, 0))],
        out_specs=(pl.BlockSpec((tm, d), lambda i: (i, 0)),
                   pl.BlockSpec((tm, LANES), lambda i: (i, 0))),
        compiler_params=_cparams(("parallel",)),
        name="ln1_forget",
    )(h, g.reshape(1, d), wf)


def _rmsnorm_body(x_ref, g_ref, o_ref):
    o_ref[...] = _rms(x_ref[...], g_ref[...]).astype(o_ref.dtype)


def _rmsnorm(h, g, out_dtype, tm=256):
    t, d = h.shape
    tm = min(tm, t)
    return pl.pallas_call(
        _rmsnorm_body,
        out_shape=jax.ShapeDtypeStruct((t, d), out_dtype),
        grid=(t // tm,),
        in_specs=[pl.BlockSpec((tm, d), lambda i: (i, 0)),
                  pl.BlockSpec((1, d), lambda i: (0, 0))],
        out_specs=pl.BlockSpec((tm, d), lambda i: (i, 0)),
        compiler_params=_cparams(("parallel",)),
        name="rmsnorm",
    )(h, g.reshape(1, d))


def _mm_body(x_ref, w_ref, o_ref):
    o_ref[...] = jnp.dot(x_ref[...], w_ref[...].astype(BF16),
                         preferred_element_type=F32).astype(o_ref.dtype)


def _matmul(x, w, out_dtype, tm, tn):
    t, k = x.shape
    n = w.shape[1]
    tm, tn = _tile(t, tm, 8), _tile(n, tn)
    return pl.pallas_call(
        _mm_body,
        out_shape=jax.ShapeDtypeStruct((t, n), out_dtype),
        grid=(t // tm, n // tn),
        in_specs=[pl.BlockSpec((tm, k), lambda i, j: (i, 0)),
                  pl.BlockSpec((k, tn), lambda i, j: (0, j))],
        out_specs=pl.BlockSpec((tm, tn), lambda i, j: (i, j)),
        compiler_params=_cparams(("parallel", "arbitrary"), VMEM_LIMIT),
        name="in_proj",
    )(x, w)


def _oproj_body(a_ref, b_ref, c_ref, d_ref, w_ref, r_ref, o_ref, *, dg):
    acc = r_ref[...]
    for part, x_ref in enumerate((a_ref, b_ref, c_ref, d_ref)):
        w = w_ref[part * dg:(part + 1) * dg, :].astype(BF16)
        acc = acc + jnp.dot(x_ref[...], w, preferred_element_type=F32)
    o_ref[...] = acc


def _out_proj(parts, w, res, tm, tn):
    t, dg = parts[0].shape
    k, n = w.shape
    tm, tn = _tile(t, tm, 8), _tile(n, tn)
    xspec = pl.BlockSpec((tm, dg), lambda i, j: (i, 0))
    return pl.pallas_call(
        functools.partial(_oproj_body, dg=dg),
        out_shape=jax.ShapeDtypeStruct((t, n), F32),
        grid=(t // tm, n // tn),
        in_specs=[xspec, xspec, xspec, xspec,
                  pl.BlockSpec((k, tn), lambda i, j: (0, j)),
                  pl.BlockSpec((tm, tn), lambda i, j: (i, j))],
        out_specs=pl.BlockSpec((tm, tn), lambda i, j: (i, j)),
        compiler_params=_cparams(("parallel", "arbitrary"), VMEM_LIMIT),
        name="out_proj",
    )(*parts, w, res)


def _silu(x):
    return x * (1.0 / (1.0 + jnp.exp(-x)))


def _ffn1_body(x_ref, w1_ref, w3_ref, o_ref):
    x = x_ref[...]
    a = jnp.dot(x, w1_ref[...].astype(BF16), preferred_element_type=F32)
    b = jnp.dot(x, w3_ref[...].astype(BF16), preferred_element_type=F32)
    o_ref[...] = (_silu(a) * b).astype(o_ref.dtype)


def _ffn1(x, w1, w3, tm, tn):
    t, k = x.shape
    n = w1.shape[1]
    tm, tn = _tile(t, tm, 8), _tile(n, tn)
    wspec = pl.BlockSpec((k, tn), lambda i, j: (0, j))
    return pl.pallas_call(
        _ffn1_body,
        out_shape=jax.ShapeDtypeStruct((t, n), BF16),
        grid=(t // tm, n // tn),
        in_specs=[pl.BlockSpec((tm, k), lambda i, j: (i, 0)), wspec, wspec],
        out_specs=pl.BlockSpec((tm, tn), lambda i, j: (i, j)),
        compiler_params=_cparams(("parallel", "arbitrary"), VMEM_LIMIT),
        name="ffn_up",
    )(x, w1, w3)


def _mm_res_body(x_ref, w_ref, r_ref, o_ref):
    o_ref[...] = r_ref[...] + jnp.dot(x_ref[...], w_ref[...].astype(BF16), preferred_element_type=F32)


def _ffn2(x, w, res, tm, tn):
    t, k = x.shape
    n = w.shape[1]
    tm, tn = _tile(t, tm, 8), _tile(n, tn)
    return pl.pallas_call(
        _mm_res_body,
        out_shape=jax.ShapeDtypeStruct((t, n), F32),
        grid=(t // tm, n // tn),
        in_specs=[pl.BlockSpec((tm, k), lambda i, j: (i, 0), pipeline_mode=pl.Buffered(1)),
                  pl.BlockSpec((k, tn), lambda i, j: (0, j)),
                  pl.BlockSpec((tm, tn), lambda i, j: (i, j))],
        out_specs=pl.BlockSpec((tm, tn), lambda i, j: (i, j)),
        compiler_params=_cparams(("parallel", "arbitrary"), VMEM_LIMIT),
        name="ffn_down",
    )(x, w, res)


def _gelu_tanh(x):
    return 0.5 * x * (1.0 + jnp.tanh(math.sqrt(2.0 / math.pi) * (x + 0.044715 * (x * x * x))))


def _sgu_body(z_ref, w_ref, b_ref, o_ref, *, dg, nh):
    rows = z_ref.shape[0]
    z = _gelu_tanh(z_ref[...].astype(F32))
    u = z[:, :dg]
    v = z[:, dg:]
    mu = jnp.mean(v, axis=-1, keepdims=True)
    vc = v - mu
    var = jnp.mean(vc * vc, axis=-1, keepdims=True)
    vn = (vc * lax.rsqrt(var + RMS_EPS)).astype(BF16)
    t_chunk = lax.broadcasted_iota(jnp.int32, (SGU_BLOCK, SGU_BLOCK), 0) // CHUNK
    s_chunk = lax.broadcasted_iota(jnp.int32, (SGU_BLOCK, SGU_BLOCK), 1) // CHUNK
    allowed = t_chunk >= s_chunk
    for hd in range(nh):
        wm = jnp.where(allowed, w_ref[hd], 0.0).astype(BF16)
        bias = b_ref[hd]
        cs = slice(hd * HEAD_DIM, (hd + 1) * HEAD_DIM)
        for blk in range(rows // SGU_BLOCK):
            rs = slice(blk * SGU_BLOCK, (blk + 1) * SGU_BLOCK)
            mixed = jnp.dot(wm, vn[rs, cs], preferred_element_type=F32) + bias
            o_ref[rs, cs] = (u[rs, cs] * mixed).astype(o_ref.dtype)


def _sgu(proj, w_s, b_s, dg, rows=256):
    t = proj.shape[0]
    nh = w_s.shape[0]
    rows = min(rows, t)
    bias = jnp.broadcast_to(b_s[:, :, None], (nh, SGU_BLOCK, HEAD_DIM))
    return pl.pallas_call(
        functools.partial(_sgu_body, dg=dg, nh=nh),
        out_shape=jax.ShapeDtypeStruct((t, dg), BF16),
        grid=(t // rows,),
        in_specs=[pl.BlockSpec((rows, 2 * dg), lambda i: (i, 0)),
                  pl.BlockSpec((nh, SGU_BLOCK, SGU_BLOCK), lambda i: (0, 0, 0)),
                  pl.BlockSpec((nh, SGU_BLOCK, HEAD_DIM), lambda i: (0, 0, 0))],
        out_specs=pl.BlockSpec((rows, dg), lambda i: (i, 0)),
        compiler_params=_cparams(("parallel",)),
        name="sgu",
    )(proj, w_s, bias)


def _conv_body(x_ref, b_ref, c_ref, xh_ref, ch_ref, w_ref, o_ref, *, tiles_per_seq):
    i = pl.program_id(0)
    rows = x_ref.shape[0]
    hcur = c_ref[...].astype(F32) * x_ref[...].astype(F32)
    halo = ch_ref[...].astype(F32) * xh_ref[...].astype(F32)
    halo = jnp.where(i % tiles_per_seq == 0, 0.0, halo)
    hp = jnp.concatenate([halo, hcur], axis=0)
    w = w_ref[...]
    y = w[CONV_WIDTH - 1:CONV_WIDTH, :] * hcur
    for tap in range(CONV_WIDTH - 1):
        off = BF16_ROWS - (CONV_WIDTH - 1) + tap
        y = y + w[tap:tap + 1, :] * hp[off:off + rows, :]
    o_ref[...] = (b_ref[...].astype(F32) * y).astype(o_ref.dtype)


def _conv(proj, conv_w, seq, dg, col0, rows=512):
    t = proj.shape[0]
    rows = min(rows, seq)
    cb = col0 // dg
    hb = rows // BF16_ROWS
    cur = lambda off: pl.BlockSpec((rows, dg), lambda i: (i, cb + off))
    halo = lambda off: pl.BlockSpec((BF16_ROWS, dg), lambda i: (jnp.maximum(i * hb - 1, 0), cb + off))
    wpad = jnp.zeros((8, dg), F32).at[:CONV_WIDTH].set(conv_w)
    return pl.pallas_call(
        functools.partial(_conv_body, tiles_per_seq=seq // rows),
        out_shape=jax.ShapeDtypeStruct((t, dg), BF16),
        grid=(t // rows,),
        in_specs=[cur(0), cur(1), cur(2), halo(0), halo(2),
                  pl.BlockSpec((8, dg), lambda i: (0, 0))],
        out_specs=pl.BlockSpec((rows, dg), lambda i: (i, 0)),
        compiler_params=_cparams(("parallel",)),
        name="short_conv",
    )(proj, proj, proj, proj, proj, wpad)


def _logsig_cumsum_body(f_ref, b_ref, c_ref, carry_ref):
    s = pl.program_id(1)

    @pl.when(s == 0)
    def _():
        carry_ref[...] = jnp.zeros_like(carry_ref)

    x = f_ref[0] + b_ref[...]
    ls = jnp.minimum(x, 0.0) - jnp.log(1.0 + jnp.exp(-jnp.abs(x)))
    rows = ls.shape[0]
    row = lax.broadcasted_iota(jnp.int32, ls.shape, 0)
    shift = 1
    while shift < rows:
        ls = ls + jnp.where(row >= shift, pltpu.roll(ls, shift, 0), 0.0)
        shift *= 2
    out = ls + carry_ref[0:1, :]
    c_ref[0] = out
    carry_ref[...] = jnp.broadcast_to(out[rows - 1:rows, :], carry_ref.shape)


def _logsig_cumsum(f, b_f, rows=512):
    bsz, seq, _ = f.shape
    rows = min(rows, seq)
    bias = jnp.zeros((1, LANES), F32).at[0, :b_f.shape[0]].set(b_f)
    return pl.pallas_call(
        _logsig_cumsum_body,
        out_shape=jax.ShapeDtypeStruct(f.shape, F32),
        grid=(bsz, seq // rows),
        in_specs=[pl.BlockSpec((1, rows, LANES), lambda b, s: (b, s, 0)),
                  pl.BlockSpec((1, LANES), lambda b, s: (0, 0))],
        out_specs=pl.BlockSpec((1, rows, LANES), lambda b, s: (b, s, 0)),
        scratch_shapes=[pltpu.VMEM((8, LANES), F32)],
        compiler_params=_cparams(("parallel", "arbitrary")),
        name="forget_cumsum",
    )(f, bias)


def _qk(q, k):
    return lax.dot_general(q, k, (((1,), (1,)), ((), ())), preferred_element_type=F32)


def _fox_body(q_ref, k_ref, v_ref, ccol_ref, crow_ref, o_ref, *, tq, scale):
    hd = pl.program_id(1)
    qi = pl.program_id(2)
    q = q_ref[0]
    lane = lax.broadcasted_iota(jnp.int32, (tq, LANES), 1)
    cq = jnp.sum(jnp.where(lane == hd, ccol_ref[0], 0.0), axis=1, keepdims=True)

    def step(j, carry, diagonal):
        m, l, acc = carry
        start = pl.multiple_of(j * tq, tq)
        k = k_ref[0, pl.ds(start, tq), :]
        v = v_ref[0, pl.ds(start, tq), :]
        s = _qk(q, k) * scale + cq - crow_ref[0, 0, pl.ds(j, 1), :]
        if diagonal:
            r = lax.broadcasted_iota(jnp.int32, (tq, tq), 0)
            c = lax.broadcasted_iota(jnp.int32, (tq, tq), 1)
            s = jnp.where(c <= r, s, NEG_INF)
        m_new = jnp.maximum(m, jnp.max(s, axis=1, keepdims=True))
        p = jnp.exp(s - m_new)
        alpha = jnp.exp(m - m_new)
        l = alpha * l + jnp.sum(p, axis=1, keepdims=True)
        acc = alpha * acc + jnp.dot(p.astype(BF16), v, preferred_element_type=F32)
        return m_new, l, acc

    init = (jnp.full((tq, 1), NEG_INF, F32), jnp.zeros((tq, 1), F32), jnp.zeros((tq, HEAD_DIM), F32))
    carry = lax.fori_loop(0, qi, lambda j, c: step(j, c, False), init)
    _, l, acc = step(qi, carry, True)
    o_ref[0] = (acc / l).astype(o_ref.dtype)


def _fox(proj3, c_col, c_row, nh, col_q, col_k, col_v, tq=512):
    bsz, seq, _ = proj3.shape
    tq = min(tq, seq)
    nq = seq // tq
    c_row = c_row.reshape(bsz, nh, nq, tq)
    kv = lambda col: pl.BlockSpec((1, seq, HEAD_DIM), lambda b, h, i: (b, 0, col + h))
    return pl.pallas_call(
        functools.partial(_fox_body, tq=tq, scale=HEAD_DIM ** -0.5),
        out_shape=jax.ShapeDtypeStruct((bsz, seq, nh * HEAD_DIM), BF16),
        grid=(bsz, nh, nq),
        in_specs=[pl.BlockSpec((1, tq, HEAD_DIM), lambda b, h, i: (b, i, col_q + h)),
                  kv(col_k), kv(col_v),
                  pl.BlockSpec((1, tq, LANES), lambda b, h, i: (b, i, 0)),
                  pl.BlockSpec((1, 1, nq, tq), lambda b, h, i: (b, h, 0, 0))],
        out_specs=pl.BlockSpec((1, tq, HEAD_DIM), lambda b, h, i: (b, i, h)),
        compiler_params=_cparams(("parallel", "parallel", "arbitrary")),
        name="fox_attention",
    )(proj3, proj3, proj3, c_col, c_row)


def _diff_body(slope_ref, q_ref, k_ref, v_ref, lam_ref, g_ref, o_ref, *, tq, scale, lam_init):
    hd = pl.program_id(1)
    qi = pl.program_id(2)
    half = HEAD_DIM // 2
    slope = slope_ref[hd]
    lv = lam_ref[...]
    lam = (jnp.exp(jnp.sum(lv[0:1] * lv[1:2], axis=1, keepdims=True))
           - jnp.exp(jnp.sum(lv[2:3] * lv[3:4], axis=1, keepdims=True)) + lam_init)
    q = q_ref[0]
    first = lax.broadcasted_iota(jnp.int32, (tq, HEAD_DIM), 1) < half
    qs = (jnp.where(first, q, jnp.zeros_like(q)), jnp.where(first, jnp.zeros_like(q), q))
    row = lax.broadcasted_iota(jnp.int32, (tq, 1), 0).astype(F32)
    col = lax.broadcasted_iota(jnp.int32, (1, tq), 1).astype(F32)

    def step(j, carry, diagonal):
        start = pl.multiple_of(j * tq, tq)
        k = k_ref[0, pl.ds(start, tq), :]
        v = v_ref[0, pl.ds(start, tq), :]
        if diagonal:
            r = lax.broadcasted_iota(jnp.int32, (tq, tq), 0)
            c = lax.broadcasted_iota(jnp.int32, (tq, tq), 1)
            bias = -slope * jnp.abs(r - c).astype(F32)
            allowed = (c // CHUNK) <= (r // CHUNK)
        else:
            gap = ((qi - j) * tq).astype(F32)
            row_bias = -slope * (row + gap)
            col_bias = slope * col
        out = []
        for mp in range(2):
            m, l, acc = carry[mp]
            s = _qk(qs[mp], k) * scale
            if diagonal:
                s = jnp.where(allowed, s + bias, NEG_INF)
            else:
                s = s + row_bias + col_bias
            m_new = jnp.maximum(m, jnp.max(s, axis=1, keepdims=True))
            p = jnp.exp(s - m_new)
            alpha = jnp.exp(m - m_new)
            l = alpha * l + jnp.sum(p, axis=1, keepdims=True)
            acc = alpha * acc + jnp.dot(p.astype(BF16), v, preferred_element_type=F32)
            out.append((m_new, l, acc))
        return tuple(out)

    one = (jnp.full((tq, 1), NEG_INF, F32), jnp.zeros((tq, 1), F32), jnp.zeros((tq, HEAD_DIM), F32))
    carry = lax.fori_loop(0, qi, lambda j, c: step(j, c, False), (one, one))
    (_, l0, a0), (_, l1, a1) = step(qi, carry, True)
    out = a0 / l0 - lam * (a1 / l1)
    o_ref[0] = (_rms(out, g_ref[...]) * (1.0 - lam_init)).astype(o_ref.dtype)


def _diff(proj3, lam_vecs, subln_g, nh, col_q, col_k, col_v, lam_init, tq=512):
    bsz, seq, _ = proj3.shape
    tq = min(tq, seq)
    nq = seq // tq
    slopes = (2.0 ** (-8.0 * jnp.arange(1, nh + 1, dtype=F32) / nh)).astype(F32)
    kv = lambda col: pl.BlockSpec((1, seq, HEAD_DIM), lambda b, h, i, s: (b, 0, col + h))
    return pl.pallas_call(
        functools.partial(_diff_body, tq=tq, scale=(HEAD_DIM // 2) ** -0.5, lam_init=lam_init),
        out_shape=jax.ShapeDtypeStruct((bsz, seq, nh * HEAD_DIM), BF16),
        grid_spec=pltpu.PrefetchScalarGridSpec(
            num_scalar_prefetch=1,
            grid=(bsz, nh, nq),
            in_specs=[pl.BlockSpec((1, tq, HEAD_DIM), lambda b, h, i, s: (b, i, col_q + h)),
                      kv(col_k), kv(col_v),
                      pl.BlockSpec(lam_vecs.shape, lambda b, h, i, s: (0, 0)),
                      pl.BlockSpec((1, HEAD_DIM), lambda b, h, i, s: (0, 0))],
            out_specs=pl.BlockSpec((1, tq, HEAD_DIM), lambda b, h, i, s: (b, i, h)),
        ),
        compiler_params=_cparams(("parallel", "parallel", "arbitrary")),
        name="diff_attention",
    )(slopes, proj3, proj3, proj3, lam_vecs, subln_g.reshape(1, HEAD_DIM))


def _router_body(h_ref, g_ref, rw_ref, info_ref, cnt_ref, carry_ref, *, n_exp):
    i = pl.program_id(0)

    @pl.when(i == 0)
    def _():
        carry_ref[...] = jnp.zeros_like(carry_ref)

    tm = h_ref.shape[0]
    n = _rms(h_ref[...], g_ref[...])
    logits = jnp.dot(n, rw_ref[...], preferred_element_type=F32, precision=lax.Precision.HIGHEST)
    lane = lax.broadcasted_iota(jnp.int32, (tm, LANES), 1).astype(F32)
    logits = jnp.where(lane < n_exp, logits, -jnp.inf)
    v1 = jnp.max(logits, axis=1, keepdims=True)
    i1 = jnp.min(jnp.where(logits == v1, lane, float(LANES)), axis=1, keepdims=True)
    rest = jnp.where(lane == i1, -jnp.inf, logits)
    v2 = jnp.max(rest, axis=1, keepdims=True)
    i2 = jnp.min(jnp.where(rest == v2, lane, float(LANES)), axis=1, keepdims=True)
    ratio = jnp.exp(v2 - v1)
    g1 = 1.0 / (1.0 + ratio)
    g2 = ratio / (1.0 + ratio)
    hit1 = lane == i1
    hit2 = lane == i2
    onehot = jnp.where(hit1 | hit2, 1.0, 0.0)
    r = lax.broadcasted_iota(jnp.int32, (tm, tm), 0)
    c = lax.broadcasted_iota(jnp.int32, (tm, tm), 1)
    before = jnp.where(c < r, 1.0, 0.0).astype(BF16)
    rank = jnp.dot(before, onehot.astype(BF16), preferred_element_type=F32) + carry_ref[0:1, :]
    rank1 = jnp.sum(jnp.where(hit1, rank, 0.0), axis=1, keepdims=True)
    rank2 = jnp.sum(jnp.where(hit2, rank, 0.0), axis=1, keepdims=True)
    total = carry_ref[0:1, :] + jnp.sum(onehot, axis=0, keepdims=True)
    carry_ref[...] = jnp.broadcast_to(total, carry_ref.shape)
    cnt_ref[...] = jnp.broadcast_to(total, cnt_ref.shape)
    fields = (i1, i2, g1, g2, rank1, rank2)
    info = jnp.zeros((tm, LANES), F32)
    for pos, val in enumerate(fields):
        info = jnp.where(lane == pos, val, info)
    info_ref[...] = info


def _router(h, g, router_w, tm=256):
    t, d = h.shape
    n_exp = router_w.shape[1]
    tm = min(tm, t)
    rw = jnp.zeros((d, LANES), F32).at[:, :n_exp].set(router_w)
    return pl.pallas_call(
        functools.partial(_router_body, n_exp=n_exp),
        out_shape=(jax.ShapeDtypeStruct((t, LANES), F32), jax.ShapeDtypeStruct((8, LANES), F32)),
        grid=(t // tm,),
        in_specs=[pl.BlockSpec((tm, d), lambda i: (i, 0)),
                  pl.BlockSpec((1, d), lambda i: (0, 0)),
                  pl.BlockSpec((d, LANES), lambda i: (0, 0))],
        out_specs=(pl.BlockSpec((tm, LANES), lambda i: (i, 0)),
                   pl.BlockSpec((8, LANES), lambda i: (0, 0))),
        scratch_shapes=[pltpu.VMEM((8, LANES), F32)],
        compiler_params=_cparams(("arbitrary",)),
        name="router",
    )(h, g.reshape(1, d), rw)


def _row_copy(src_hbm, dst, sem, src_row, dst_row):
    return pltpu.make_async_copy(src_hbm.at[pl.ds(src_row, 1)], dst.at[pl.ds(dst_row, 1)], sem)


def _gather_norm_body(idx_ref, h_hbm, g_ref, o_ref, buf, sem):
    rows = buf.shape[0]

    def start(r, c):
        _row_copy(h_hbm, buf, sem, idx_ref[0, 0, r], r).start()
        return c

    def wait(r, c):
        _row_copy(h_hbm, buf, sem, 0, r).wait()
        return c

    lax.fori_loop(0, rows, start, 0)
    lax.fori_loop(0, rows, wait, 0)
    o_ref[...] = _rms(buf[...], g_ref[...]).astype(o_ref.dtype)


def _gather_norm(h, g, row_token, rows=256):
    d = h.shape[1]
    total = row_token.shape[0]
    idx = row_token.reshape(total // rows, 1, rows)
    return pl.pallas_call(
        _gather_norm_body,
        out_shape=jax.ShapeDtypeStruct((total, d), BF16),
        grid=(total // rows,),
        in_specs=[pl.BlockSpec((1, 1, rows), lambda i: (i, 0, 0), memory_space=pltpu.SMEM),
                  pl.BlockSpec(memory_space=pl.ANY),
                  pl.BlockSpec((1, d), lambda i: (0, 0))],
        out_specs=pl.BlockSpec((rows, d), lambda i: (i, 0)),
        scratch_shapes=[pltpu.VMEM((rows, d), F32), pltpu.SemaphoreType.DMA],
        compiler_params=_cparams(("arbitrary",)),
        name="moe_gather",
    )(idx, h, g.reshape(1, d))


def _group_first(te_ref, tv_ref, r):
    prev = te_ref[jnp.maximum(r - 1, 0)]
    return (r == 0) | (te_ref[r] != prev)


def _gffn1_body(te_ref, tv_ref, tx_ref, x_ref, w1_ref, w3_ref, o_ref, w1b, w3b):
    r = pl.program_id(1)

    @pl.when(_group_first(te_ref, tv_ref, r))
    def _():
        w1b[...] = w1_ref[0].astype(BF16)
        w3b[...] = w3_ref[0].astype(BF16)

    @pl.when(tv_ref[r] == 1)
    def _():
        x = x_ref[...]
        a = jnp.dot(x, w1b[...], preferred_element_type=F32)
        b = jnp.dot(x, w3b[...], preferred_element_type=F32)
        o_ref[...] = (_silu(a) * b).astype(o_ref.dtype)

    @pl.when(tv_ref[r] == 0)
    def _():
        o_ref[...] = jnp.zeros_like(o_ref)


def _gmm_body(te_ref, tv_ref, tx_ref, x_ref, w_ref, o_ref, wb):
    r = pl.program_id(1)

    @pl.when(_group_first(te_ref, tv_ref, r))
    def _():
        wb[...] = w_ref[0].astype(BF16)

    @pl.when(tv_ref[r] == 1)
    def _():
        o_ref[...] = jnp.dot(x_ref[...], wb[...], preferred_element_type=F32).astype(o_ref.dtype)

    @pl.when(tv_ref[r] == 0)
    def _():
        o_ref[...] = jnp.zeros_like(o_ref)


def _grouped(body, name, x, ws, tables, tr, tn, out_dtype):
    total, k = x.shape
    n = ws[0].shape[2]
    tn = _tile(n, tn)
    n_tiles = total // tr
    wspec = pl.BlockSpec((1, k, tn), lambda j, r, te, tv, tx: (te[r], 0, j))
    return pl.pallas_call(
        body,
        out_shape=jax.ShapeDtypeStruct((total, n), out_dtype),
        grid_spec=pltpu.PrefetchScalarGridSpec(
            num_scalar_prefetch=3,
            grid=(n // tn, n_tiles),
            in_specs=[pl.BlockSpec((tr, k), lambda j, r, te, tv, tx: (tx[r], 0))] + [wspec] * len(ws),
            out_specs=pl.BlockSpec((tr, tn), lambda j, r, te, tv, tx: (r, j)),
            scratch_shapes=[pltpu.VMEM((k, tn), BF16)] * len(ws),
        ),
        compiler_params=_cparams(("arbitrary", "arbitrary"), VMEM_LIMIT),
        name=name,
    )(*tables, x, *ws)


def _combine_body(d0_ref, d1_ref, y_hbm, h_ref, info_ref, g_ref, o_ref, buf0, buf1, sem):
    rows = h_ref.shape[0]

    def start(r, c):
        _row_copy(y_hbm, buf0, sem, d0_ref[0, 0, r], r).start()
        _row_copy(y_hbm, buf1, sem, d1_ref[0, 0, r], r).start()
        return c

    def wait(r, c):
        _row_copy(y_hbm, buf0, sem, 0, r).wait()
        _row_copy(y_hbm, buf1, sem, 0, r).wait()
        return c

    lax.fori_loop(0, rows, start, 0)
    lax.fori_loop(0, rows, wait, 0)
    info = info_ref[...]
    lane = lax.broadcasted_iota(jnp.int32, info.shape, 1)
    g0 = jnp.sum(jnp.where(lane == 2, info, 0.0), axis=1, keepdims=True)
    g1 = jnp.sum(jnp.where(lane == 3, info, 0.0), axis=1, keepdims=True)
    out = h_ref[...] + g0 * buf0[...] + g1 * buf1[...]
    o_ref[...] = _rms(out, g_ref[...])


def _combine_norm(h, y, dest, info, g, rows=256):
    t, d = h.shape
    rows = min(rows, t)
    d0 = dest[:, 0].reshape(t // rows, 1, rows)
    d1 = dest[:, 1].reshape(t // rows, 1, rows)
    ispec = pl.BlockSpec((1, 1, rows), lambda i: (i, 0, 0), memory_space=pltpu.SMEM)
    return pl.pallas_call(
        _combine_body,
        out_shape=jax.ShapeDtypeStruct((t, d), F32),
        grid=(t // rows,),
        in_specs=[ispec, ispec,
                  pl.BlockSpec(memory_space=pl.ANY),
                  pl.BlockSpec((rows, d), lambda i: (i, 0)),
                  pl.BlockSpec((rows, LANES), lambda i: (i, 0)),
                  pl.BlockSpec((1, d), lambda i: (0, 0))],
        out_specs=pl.BlockSpec((rows, d), lambda i: (i, 0)),
        scratch_shapes=[pltpu.VMEM((rows, d), F32), pltpu.VMEM((rows, d), F32), pltpu.SemaphoreType.DMA],
        compiler_params=_cparams(("arbitrary",)),
        name="moe_combine",
    )(d0, d1, y, h, info, g.reshape(1, d))


def _moe_final(h, ln_g, router_w, w1, w3, w2, final_g, tr=512, tn=512):
    t, d = h.shape
    n_exp = router_w.shape[1]
    tr = min(tr, t)
    info, cnt = _router(h, ln_g, router_w)

    expert = info[:, 0:TOP_K].astype(jnp.int32)
    rank = info[:, 4:4 + TOP_K].astype(jnp.int32)
    counts = cnt[0, :n_exp].astype(jnp.int32)
    tiles_per = (counts + tr - 1) // tr
    tile_end = jnp.cumsum(tiles_per)
    tile_start = tile_end - tiles_per
    n_tiles = TOP_K * t // tr + n_exp
    total = n_tiles * tr
    dest = tile_start[expert] * tr + rank
    token = jnp.broadcast_to(jnp.arange(t, dtype=jnp.int32)[:, None], (t, TOP_K))
    row_token = jnp.zeros((total,), jnp.int32).at[dest.reshape(-1)].set(token.reshape(-1))
    tile_id = jnp.arange(n_tiles, dtype=jnp.int32)
    n_valid = tile_end[n_exp - 1]
    last = jnp.maximum(n_valid - 1, 0)
    tile_valid = (tile_id < n_valid).astype(jnp.int32)
    tile_src = jnp.minimum(tile_id, last)
    tile_expert = jnp.minimum(jnp.searchsorted(tile_end, tile_src, side="right"), n_exp - 1).astype(jnp.int32)
    tables = (tile_expert, tile_valid, tile_src)

    xs = _gather_norm(h, ln_g, row_token)
    gs = _grouped(_gffn1_body, "moe_up", xs, (w1, w3), tables, tr, tn, BF16)
    ys = _grouped(_gmm_body, "moe_down", gs, (w2,), tables, tr, tn, F32)
    return _combine_norm(h, ys, dest, info, final_g)


def kernel(x, ln1_g, w_in, b_f, w_s, b_s, conv_w, lam_q1, lam_k1, lam_q2, lam_k2, subln_g, w_o, ln2_g,
           ffn_w1, ffn_w3, ffn_w2, router_w, moe_w1, moe_w3, moe_w2, final_g):
    bsz, seq, d = x.shape
    depth = ln1_g.shape[0]
    nh = b_f.shape[1]
    dg = nh * HEAD_DIM
    t = bsz * seq
    assert w_in.shape[2] == 11 * dg + nh and d == 4 * dg and depth % 2 == 0
    f0 = 5 * dg
    blk = lambda cols: cols // HEAD_DIM
    col_fq, col_fk, col_fv = blk(2 * dg), blk(3 * dg), blk(4 * dg)
    col_dq, col_dk, col_dv = blk(8 * dg), blk(9 * dg), blk(10 * dg)

    h = x.reshape(t, d)
    out = None
    for i in range(depth):
        w_main = jnp.concatenate([w_in[i, :, :f0], w_in[i, :, f0 + nh:]], axis=1).astype(BF16)
        w_f = jnp.zeros((d, LANES), BF16).at[:, :nh].set(w_in[i, :, f0:f0 + nh].astype(BF16))

        n, flog = _ln1(h, ln1_g[i], w_f)
        proj = _matmul(n, w_main, BF16, tm=1024, tn=1024)
        proj3 = proj.reshape(bsz, seq, proj.shape[1])

        out_a = _sgu(proj, w_s[i], b_s[i], dg)
        c_col = _logsig_cumsum(flog.reshape(bsz, seq, LANES), b_f[i])
        c_row = jnp.transpose(c_col[:, :, :nh], (0, 2, 1))
        out_b = _fox(proj3, c_col, c_row, nh, col_fq, col_fk, col_fv).reshape(t, dg)
        out_c = _conv(proj, conv_w[i], seq, dg, 5 * dg)
        lam_init = 0.8 - 0.6 * math.exp(-0.3 * i)
        lam_vecs = jnp.stack([lam_q1[i], lam_k1[i], lam_q2[i], lam_k2[i]]).astype(F32)
        out_d = _diff(proj3, lam_vecs, subln_g[i], nh, col_dq, col_dk, col_dv, lam_init).reshape(t, dg)

        h = _out_proj((out_a, out_b, out_c, out_d), w_o[i], h, tm=1024, tn=512)

        j = i // 2
        if i % 2 == 0:
            n2 = _rmsnorm(h, ln2_g[i], BF16)
            g = _ffn1(n2, ffn_w1[j], ffn_w3[j], tm=1024, tn=256)
            h = _ffn2(g, ffn_w2[j], h, tm=1024, tn=256)
        else:
            assert i == depth - 1
            out = _moe_final(h, ln2_g[i], router_w[j], moe_w1[j], moe_w3[j], moe_w2[j], final_g)
    return out.reshape(bsz, seq, d)
```

```python
import functools
import math

import jax
import jax.numpy as jnp
from jax import lax
from jax.experimental import pallas as pl
from jax.experimental.pallas import tpu as pltpu

F32 = jnp.float32
BF16 = jnp.bfloat16

HEAD_DIM = 128
CHUNK = 64
SGU_BLOCK = 128
CONV_WIDTH = 3
TOP_K = 2
RMS_EPS = 1e-6
NEG_INF = -1e30
LOG2E = 1.4426950408889634
LANES = 128
BF16_ROWS = 16
ROW_DMA_UNROLL = 8
VMEM_LIMIT = 58 * 1024 * 1024


def _cparams(sem, vmem=None):
    return pltpu.CompilerParams(dimension_semantics=sem, vmem_limit_bytes=vmem)


def _tile(dim, target, unit=LANES):
    if dim <= target:
        return dim
    best = max(c for c in range(unit, target + 1, unit) if dim % c == 0)
    return best


def _rms(x, g):
    ms = jnp.mean(x * x, axis=-1, keepdims=True)
    return x * lax.rsqrt(ms + RMS_EPS) * g


def _ln1_body(x_ref, g_ref, wf_ref, n_ref, f_ref):
    nb = _rms(x_ref[...], g_ref[...]).astype(BF16)
    n_ref[...] = nb
    f_ref[...] = jnp.dot(nb, wf_ref[...], preferred_element_type=F32)


def _ln1(h, g, wf, tm=256):
    t, d = h.shape
    tm = min(tm, t)
    return pl.pallas_call(
        _ln1_body,
        out_shape=(jax.ShapeDtypeStruct((t, d), BF16), jax.ShapeDtypeStruct((t, LANES), F32)),
        grid=(t // tm,),
        in_specs=[pl.BlockSpec((tm, d), lambda i: (i, 0)),
                  pl.BlockSpec((1, d), lambda i: (0, 0)),
                  pl.BlockSpec((d, LANES), lambda i: (0, 0))],
        out_specs=(pl.BlockSpec((tm, d), lambda i: (i, 0)),
                   pl.BlockSpec((tm, LANES), lambda i: (i, 0))),
        compiler_params=_cparams(("parallel",)),
        name="ln1_forget",
    )(h, g.reshape(1, d), wf)


def _rmsnorm_body(x_ref, g_ref, o_ref):
    o_ref[...] = _rms(x_ref[...], g_ref[...]).astype(o_ref.dtype)


def _rmsnorm(h, g, out_dtype, tm=256):
    t, d = h.shape
    tm = min(tm, t)
    return pl.pallas_call(
        _rmsnorm_body,
        out_shape=jax.ShapeDtypeStruct((t, d), out_dtype),
        grid=(t // tm,),
        in_specs=[pl.BlockSpec((tm, d), lambda i: (i, 0)),
                  pl.BlockSpec((1, d), lambda i: (0, 0))],
        out_specs=pl.BlockSpec((tm, d), lambda i: (i, 0)),
        compiler_params=_cparams(("parallel",)),
        name="rmsnorm",
    )(h, g.reshape(1, d))


def _mm_body(x_ref, w_ref, o_ref):
    o_ref[...] = jnp.dot(x_ref[...], w_ref[...].astype(BF16),
                         preferred_element_type=F32).astype(o_ref.dtype)


def _matmul(x, w, out_dtype, tm, tn):
    t, k = x.shape
    n = w.shape[1]
    tm, tn = _tile(t, tm, 8), _tile(n, tn)
    return pl.pallas_call(
        _mm_body,
        out_shape=jax.ShapeDtypeStruct((t, n), out_dtype),
        grid=(t // tm, n // tn),
        in_specs=[pl.BlockSpec((tm, k), lambda i, j: (i, 0)),
                  pl.BlockSpec((k, tn), lambda i, j: (0, j))],
        out_specs=pl.BlockSpec((tm, tn), lambda i, j: (i, j)),
        compiler_params=_cparams(("parallel", "arbitrary"), VMEM_LIMIT),
        name="in_proj",
    )(x, w)


def _oproj_body(a_ref, b_ref, c_ref, d_ref, w_ref, r_ref, o_ref, *, dg):
    acc = r_ref[...]
    for part, x_ref in enumerate((a_ref, b_ref, c_ref, d_ref)):
        w = w_ref[part * dg:(part + 1) * dg, :].astype(BF16)
        acc = acc + jnp.dot(x_ref[...], w, preferred_element_type=F32)
    o_ref[...] = acc


def _out_proj(parts, w, res, tm, tn):
    t, dg = parts[0].shape
    k, n = w.shape
    tm, tn = _tile(t, tm, 8), _tile(n, tn)
    xspec = pl.BlockSpec((tm, dg), lambda i, j: (i, 0))
    return pl.pallas_call(
        functools.partial(_oproj_body, dg=dg),
        out_shape=jax.ShapeDtypeStruct((t, n), F32),
        grid=(t // tm, n // tn),
        in_specs=[xspec, xspec, xspec, xspec,
                  pl.BlockSpec((k, tn), lambda i, j: (0, j)),
                  pl.BlockSpec((tm, tn), lambda i, j: (i, j))],
        out_specs=pl.BlockSpec((tm, tn), lambda i, j: (i, j)),
        compiler_params=_cparams(("parallel", "arbitrary"), VMEM_LIMIT),
        name="out_proj",
    )(*parts, w, res)


def _silu(x):
    return x * (1.0 / (1.0 + jnp.exp(-x)))


def _ffn1_body(x_ref, w1_ref, w3_ref, o_ref):
    x = x_ref[...]
    a = jnp.dot(x, w1_ref[...].astype(BF16), preferred_element_type=F32)
    b = jnp.dot(x, w3_ref[...].astype(BF16), preferred_element_type=F32)
    o_ref[...] = (_silu(a) * b).astype(o_ref.dtype)


def _ffn1(x, w1, w3, tm, tn):
    t, k = x.shape
    n = w1.shape[1]
    tm, tn = _tile(t, tm, 8), _tile(n, tn)
    wspec = pl.BlockSpec((k, tn), lambda i, j: (0, j))
    return pl.pallas_call(
        _ffn1_body,
        out_shape=jax.ShapeDtypeStruct((t, n), BF16),
        grid=(t // tm, n // tn),
        in_specs=[pl.BlockSpec((tm, k), lambda i, j: (i, 0)), wspec, wspec],
        out_specs=pl.BlockSpec((tm, tn), lambda i, j: (i, j)),
        compiler_params=_cparams(("parallel", "arbitrary"), VMEM_LIMIT),
        name="ffn_up",
    )(x, w1, w3)


def _mm_res_body(x_ref, w_ref, r_ref, o_ref):
    o_ref[...] = r_ref[...] + jnp.dot(x_ref[...], w_ref[...].astype(BF16), preferred_element_type=F32)


def _ffn2(x, w, res, tm, tn):
    t, k = x.shape
    n = w.shape[1]
    tm, tn = _tile(t, tm, 8), _tile(n, tn)
    return pl.pallas_call(
        _mm_res_body,
        out_shape=jax.ShapeDtypeStruct((t, n), F32),
        grid=(t // tm, n // tn),
        in_specs=[pl.BlockSpec((tm, k), lambda i, j: (i, 0), pipeline_mode=pl.Buffered(1)),
                  pl.BlockSpec((k, tn), lambda i, j: (0, j)),
                  pl.BlockSpec((tm, tn), lambda i, j: (i, j))],
        out_specs=pl.BlockSpec((tm, tn), lambda i, j: (i, j)),
        compiler_params=_cparams(("parallel", "arbitrary"), VMEM_LIMIT),
        name="ffn_down",
    )(x, w, res)


def _gelu_tanh(x):
    return 0.5 * x * (1.0 + jnp.tanh(math.sqrt(2.0 / math.pi) * (x + 0.044715 * (x * x * x))))


def _sgu_body(z_ref, w_ref, b_ref, o_ref, *, dg, nh):
    rows = z_ref.shape[0]
    z = _gelu_tanh(z_ref[...].astype(F32))
    u = z[:, :dg]
    v = z[:, dg:]
    mu = jnp.mean(v, axis=-1, keepdims=True)
    vc = v - mu
    var = jnp.mean(vc * vc, axis=-1, keepdims=True)
    vn = (vc * lax.rsqrt(var + RMS_EPS)).astype(BF16)
    t_chunk = lax.broadcasted_iota(jnp.int32, (SGU_BLOCK, SGU_BLOCK), 0) // CHUNK
    s_chunk = lax.broadcasted_iota(jnp.int32, (SGU_BLOCK, SGU_BLOCK), 1) // CHUNK
    allowed = t_chunk >= s_chunk
    for hd in range(nh):
        wm = jnp.where(allowed, w_ref[hd], 0.0).astype(BF16)
        bias = b_ref[hd]
        cs = slice(hd * HEAD_DIM, (hd + 1) * HEAD_DIM)
        for blk in range(rows // SGU_BLOCK):
            rs = slice(blk * SGU_BLOCK, (blk + 1) * SGU_BLOCK)
            mixed = jnp.dot(wm, vn[rs, cs], preferred_element_type=F32) + bias
            o_ref[rs, cs] = (u[rs, cs] * mixed).astype(o_ref.dtype)


def _sgu(proj, w_s, b_s, dg, rows=256):
    t = proj.shape[0]
    nh = w_s.shape[0]
    rows = min(rows, t)
    bias = jnp.broadcast_to(b_s[:, :, None], (nh, SGU_BLOCK, HEAD_DIM))
    return pl.pallas_call(
        functools.partial(_sgu_body, dg=dg, nh=nh),
        out_shape=jax.ShapeDtypeStruct((t, dg), BF16),
        grid=(t // rows,),
        in_specs=[pl.BlockSpec((rows, 2 * dg), lambda i: (i, 0)),
                  pl.BlockSpec((nh, SGU_BLOCK, SGU_BLOCK), lambda i: (0, 0, 0)),
                  pl.BlockSpec((nh, SGU_BLOCK, HEAD_DIM), lambda i: (0, 0, 0))],
        out_specs=pl.BlockSpec((rows, dg), lambda i: (i, 0)),
        compiler_params=_cparams(("parallel",)),
        name="sgu",
    )(proj, w_s, bias)


def _conv_body(x_ref, b_ref, c_ref, xh_ref, ch_ref, w_ref, o_ref, *, tiles_per_seq):
    i = pl.program_id(0)
    rows = x_ref.shape[0]
    hcur = c_ref[...].astype(F32) * x_ref[...].astype(F32)
    halo = ch_ref[...].astype(F32) * xh_ref[...].astype(F32)
    halo = jnp.where(i % tiles_per_seq == 0, 0.0, halo)
    hp = jnp.concatenate([halo, hcur], axis=0)
    w = w_ref[...]
    y = w[CONV_WIDTH - 1:CONV_WIDTH, :] * hcur
    for tap in range(CONV_WIDTH - 1):
        off = BF16_ROWS - (CONV_WIDTH - 1) + tap
        y = y + w[tap:tap + 1, :] * hp[off:off + rows, :]
    o_ref[...] = (b_ref[...].astype(F32) * y).astype(o_ref.dtype)


def _conv(proj, conv_w, seq, dg, col0, rows=512):
    t = proj.shape[0]
    rows = min(rows, seq)
    cb = col0 // dg
    hb = rows // BF16_ROWS
    cur = lambda off: pl.BlockSpec((rows, dg), lambda i: (i, cb + off))
    halo = lambda off: pl.BlockSpec((BF16_ROWS, dg), lambda i: (jnp.maximum(i * hb - 1, 0), cb + off))
    wpad = jnp.zeros((8, dg), F32).at[:CONV_WIDTH].set(conv_w)
    return pl.pallas_call(
        functools.partial(_conv_body, tiles_per_seq=seq // rows),
        out_shape=jax.ShapeDtypeStruct((t, dg), BF16),
        grid=(t // rows,),
        in_specs=[cur(0), cur(1), cur(2), halo(0), halo(2),
                  pl.BlockSpec((8, dg), lambda i: (0, 0))],
        out_specs=pl.BlockSpec((rows, dg), lambda i: (i, 0)),
        compiler_params=_cparams(("parallel",)),
        name="short_conv",
    )(proj, proj, proj, proj, proj, wpad)


def _logsig_cumsum_body(f_ref, b_ref, c_ref, carry_ref):
    s = pl.program_id(1)

    @pl.when(s == 0)
    def _():
        carry_ref[...] = jnp.zeros_like(carry_ref)

    x = f_ref[0] + b_ref[...]
    ls = jnp.minimum(x, 0.0) - jnp.log(1.0 + jnp.exp(-jnp.abs(x)))
    rows = ls.shape[0]
    row = lax.broadcasted_iota(jnp.int32, ls.shape, 0)
    shift = 1
    while shift < rows:
        ls = ls + jnp.where(row >= shift, pltpu.roll(ls, shift, 0), 0.0)
        shift *= 2
    out = ls + carry_ref[0:1, :]
    c_ref[0] = out
    carry_ref[...] = jnp.broadcast_to(out[rows - 1:rows, :], carry_ref.shape)


def _logsig_cumsum(f, b_f, rows=512):
    bsz, seq, _ = f.shape
    rows = min(rows, seq)
    bias = jnp.zeros((1, LANES), F32).at[0, :b_f.shape[0]].set(b_f)
    return pl.pallas_call(
        _logsig_cumsum_body,
        out_shape=jax.ShapeDtypeStruct(f.shape, F32),
        grid=(bsz, seq // rows),
        in_specs=[pl.BlockSpec((1, rows, LANES), lambda b, s: (b, s, 0)),
                  pl.BlockSpec((1, LANES), lambda b, s: (0, 0))],
        out_specs=pl.BlockSpec((1, rows, LANES), lambda b, s: (b, s, 0)),
        scratch_shapes=[pltpu.VMEM((8, LANES), F32)],
        compiler_params=_cparams(("parallel", "arbitrary")),
        name="forget_cumsum",
    )(f, bias)


def _qk(q, k):
    return lax.dot_general(q, k, (((1,), (1,)), ((), ())), preferred_element_type=F32)


def _online_softmax(t, row_term, v, carry):
    m, l, acc = carry
    m_new = jnp.maximum(m, jnp.max(t, axis=1, keepdims=True) + row_term)
    p = jnp.exp2(t - (m_new - row_term))
    alpha = jnp.exp2(m - m_new)
    l = alpha * l + jnp.sum(p, axis=1, keepdims=True)
    acc = alpha * acc + jnp.dot(p.astype(BF16), v, preferred_element_type=F32)
    return m_new, l, acc


def _below_diagonal(step, n_blocks, carry):
    def pair(p, c):
        return step(2 * p + 1, step(2 * p, c, False), False)

    carry = lax.fori_loop(0, n_blocks // 2, pair, carry)
    return lax.fori_loop(2 * (n_blocks // 2), n_blocks, lambda j, c: step(j, c, False), carry)


def _softmax_init(rows):
    return (jnp.full((rows, 1), NEG_INF, F32), jnp.zeros((rows, 1), F32), jnp.zeros((rows, HEAD_DIM), F32))


def _fox_body(q_ref, k_ref, v_ref, ccol_ref, crow_ref, o_ref, *, tq, scale):
    hd = pl.program_id(1)
    qi = pl.program_id(2)
    q = q_ref[0]
    lane = lax.broadcasted_iota(jnp.int32, (tq, LANES), 1)
    cq = jnp.sum(jnp.where(lane == hd, ccol_ref[0], 0.0), axis=1, keepdims=True) * LOG2E

    def step(j, carry, diagonal):
        start = pl.multiple_of(j * tq, tq)
        k = k_ref[0, pl.ds(start, tq), :]
        v = v_ref[0, pl.ds(start, tq), :]
        t = _qk(q, k) * (scale * LOG2E) - crow_ref[0, 0, pl.ds(j, 1), :] * LOG2E
        if diagonal:
            r = lax.broadcasted_iota(jnp.int32, (tq, tq), 0)
            c = lax.broadcasted_iota(jnp.int32, (tq, tq), 1)
            t = jnp.where(c <= r, t, NEG_INF)
        return _online_softmax(t, cq, v, carry)

    _, l, acc = step(qi, _below_diagonal(step, qi, _softmax_init(tq)), True)
    o_ref[0] = (acc / l).astype(o_ref.dtype)


def _fox(proj3, c_col, c_row, nh, col_q, col_k, col_v, tq=512):
    bsz, seq, _ = proj3.shape
    tq = min(tq, seq)
    nq = seq // tq
    c_row = c_row.reshape(bsz, nh, nq, tq)
    kv = lambda col: pl.BlockSpec((1, seq, HEAD_DIM), lambda b, h, i: (b, 0, col + h))
    return pl.pallas_call(
        functools.partial(_fox_body, tq=tq, scale=HEAD_DIM ** -0.5),
        out_shape=jax.ShapeDtypeStruct((bsz, seq, nh * HEAD_DIM), BF16),
        grid=(bsz, nh, nq),
        in_specs=[pl.BlockSpec((1, tq, HEAD_DIM), lambda b, h, i: (b, i, col_q + h)),
                  kv(col_k), kv(col_v),
                  pl.BlockSpec((1, tq, LANES), lambda b, h, i: (b, i, 0)),
                  pl.BlockSpec((1, 1, nq, tq), lambda b, h, i: (b, h, 0, 0))],
        out_specs=pl.BlockSpec((1, tq, HEAD_DIM), lambda b, h, i: (b, i, h)),
        compiler_params=_cparams(("parallel", "parallel", "arbitrary")),
        name="fox_attention",
    )(proj3, proj3, proj3, c_col, c_row)


def _diff_body(slope_ref, q_ref, k_ref, v_ref, lam_ref, g_ref, o_ref, *, tq, scale, lam_init):
    hd = pl.program_id(1)
    qi = pl.program_id(2)
    half = HEAD_DIM // 2
    slope = slope_ref[hd] * LOG2E
    lv = lam_ref[...]
    lam = (jnp.exp(jnp.sum(lv[0:1] * lv[1:2], axis=1, keepdims=True))
           - jnp.exp(jnp.sum(lv[2:3] * lv[3:4], axis=1, keepdims=True)) + lam_init)
    q = q_ref[0]
    first = lax.broadcasted_iota(jnp.int32, (tq, HEAD_DIM), 1) < half
    zero = jnp.zeros_like(q)
    q2 = jnp.concatenate([jnp.where(first, q, zero), jnp.where(first, zero, q)], axis=0)
    row = lax.broadcasted_iota(jnp.int32, (2 * tq, 1), 0)
    row = jnp.where(row >= tq, row - tq, row)
    col_bias = slope * lax.broadcasted_iota(jnp.int32, (1, tq), 1).astype(F32)

    def step(j, carry, diagonal):
        start = pl.multiple_of(j * tq, tq)
        k = k_ref[0, pl.ds(start, tq), :]
        v = v_ref[0, pl.ds(start, tq), :]
        t = _qk(q2, k) * (scale * LOG2E)
        if diagonal:
            c = lax.broadcasted_iota(jnp.int32, (2 * tq, tq), 1)
            t = jnp.where((c // CHUNK) <= (row // CHUNK), t - slope * jnp.abs(row - c).astype(F32), NEG_INF)
            row_term = jnp.zeros((2 * tq, 1), F32)
        else:
            t = t + col_bias
            row_term = -slope * (row + (qi - j) * tq).astype(F32)
        return _online_softmax(t, row_term, v, carry)

    _, l, acc = step(qi, _below_diagonal(step, qi, _softmax_init(2 * tq)), True)
    a = acc / l
    out = a[:tq] - lam * a[tq:]
    o_ref[0] = (_rms(out, g_ref[...]) * (1.0 - lam_init)).astype(o_ref.dtype)


def _diff(proj3, lam_vecs, subln_g, nh, col_q, col_k, col_v, lam_init, tq=512):
    bsz, seq, _ = proj3.shape
    tq = min(tq, seq)
    nq = seq // tq
    slopes = (2.0 ** (-8.0 * jnp.arange(1, nh + 1, dtype=F32) / nh)).astype(F32)
    kv = lambda col: pl.BlockSpec((1, seq, HEAD_DIM), lambda b, h, i, s: (b, 0, col + h))
    return pl.pallas_call(
        functools.partial(_diff_body, tq=tq, scale=(HEAD_DIM // 2) ** -0.5, lam_init=lam_init),
        out_shape=jax.ShapeDtypeStruct((bsz, seq, nh * HEAD_DIM), BF16),
        grid_spec=pltpu.PrefetchScalarGridSpec(
            num_scalar_prefetch=1,
            grid=(bsz, nh, nq),
            in_specs=[pl.BlockSpec((1, tq, HEAD_DIM), lambda b, h, i, s: (b, i, col_q + h)),
                      kv(col_k), kv(col_v),
                      pl.BlockSpec(lam_vecs.shape, lambda b, h, i, s: (0, 0)),
                      pl.BlockSpec((1, HEAD_DIM), lambda b, h, i, s: (0, 0))],
            out_specs=pl.BlockSpec((1, tq, HEAD_DIM), lambda b, h, i, s: (b, i, h)),
        ),
        compiler_params=_cparams(("parallel", "parallel", "arbitrary")),
        name="diff_attention",
    )(slopes, proj3, proj3, proj3, lam_vecs, subln_g.reshape(1, HEAD_DIM))


def _router_body(h_ref, g_ref, rw_ref, info_ref, cnt_ref, carry_ref, *, n_exp):
    i = pl.program_id(0)

    @pl.when(i == 0)
    def _():
        carry_ref[...] = jnp.zeros_like(carry_ref)

    tm = h_ref.shape[0]
    n = _rms(h_ref[...], g_ref[...])
    logits = jnp.dot(n, rw_ref[...], preferred_element_type=F32, precision=lax.Precision.HIGHEST)
    lane = lax.broadcasted_iota(jnp.int32, (tm, LANES), 1).astype(F32)
    logits = jnp.where(lane < n_exp, logits, -jnp.inf)
    v1 = jnp.max(logits, axis=1, keepdims=True)
    i1 = jnp.min(jnp.where(logits == v1, lane, float(LANES)), axis=1, keepdims=True)
    rest = jnp.where(lane == i1, -jnp.inf, logits)
    v2 = jnp.max(rest, axis=1, keepdims=True)
    i2 = jnp.min(jnp.where(rest == v2, lane, float(LANES)), axis=1, keepdims=True)
    ratio = jnp.exp(v2 - v1)
    g1 = 1.0 / (1.0 + ratio)
    g2 = ratio / (1.0 + ratio)
    hit1 = lane == i1
    hit2 = lane == i2
    onehot = jnp.where(hit1 | hit2, 1.0, 0.0)
    r = lax.broadcasted_iota(jnp.int32, (tm, tm), 0)
    c = lax.broadcasted_iota(jnp.int32, (tm, tm), 1)
    before = jnp.where(c < r, 1.0, 0.0).astype(BF16)
    rank = jnp.dot(before, onehot.astype(BF16), preferred_element_type=F32) + carry_ref[0:1, :]
    rank1 = jnp.sum(jnp.where(hit1, rank, 0.0), axis=1, keepdims=True)
    rank2 = jnp.sum(jnp.where(hit2, rank, 0.0), axis=1, keepdims=True)
    total = carry_ref[0:1, :] + jnp.sum(onehot, axis=0, keepdims=True)
    carry_ref[...] = jnp.broadcast_to(total, carry_ref.shape)
    cnt_ref[...] = jnp.broadcast_to(total, cnt_ref.shape)
    fields = (i1, i2, g1, g2, rank1, rank2)
    info = jnp.zeros((tm, LANES), F32)
    for pos, val in enumerate(fields):
        info = jnp.where(lane == pos, val, info)
    info_ref[...] = info


def _router(h, g, router_w, tm=256):
    t, d = h.shape
    n_exp = router_w.shape[1]
    tm = min(tm, t)
    rw = jnp.zeros((d, LANES), F32).at[:, :n_exp].set(router_w)
    return pl.pallas_call(
        functools.partial(_router_body, n_exp=n_exp),
        out_shape=(jax.ShapeDtypeStruct((t, LANES), F32), jax.ShapeDtypeStruct((8, LANES), F32)),
        grid=(t // tm,),
        in_specs=[pl.BlockSpec((tm, d), lambda i: (i, 0)),
                  pl.BlockSpec((1, d), lambda i: (0, 0)),
                  pl.BlockSpec((d, LANES), lambda i: (0, 0))],
        out_specs=(pl.BlockSpec((tm, LANES), lambda i: (i, 0)),
                   pl.BlockSpec((8, LANES), lambda i: (0, 0))),
        scratch_shapes=[pltpu.VMEM((8, LANES), F32)],
        compiler_params=_cparams(("arbitrary",)),
        name="router",
    )(h, g.reshape(1, d), rw)


def _row_copy(src_hbm, dst, sem, src_row, dst_row):
    return pltpu.make_async_copy(src_hbm.at[pl.ds(src_row, 1)], dst.at[pl.ds(dst_row, 1)], sem)


def _start_rows(src_hbm, idx_ref, dst, sem):
    def body(r, c):
        _row_copy(src_hbm, dst, sem, idx_ref[0, 0, r], r).start()
        return c

    lax.fori_loop(0, dst.shape[0], body, 0, unroll=ROW_DMA_UNROLL)


def _wait_rows(src_hbm, dst, sem):
    def body(r, c):
        _row_copy(src_hbm, dst, sem, 0, r).wait()
        return c

    lax.fori_loop(0, dst.shape[0], body, 0, unroll=ROW_DMA_UNROLL)


def _prefetch_schedule(issue):
    i = pl.program_id(0)
    slot = i % 2

    @pl.when(i == 0)
    def _():
        issue(False, 0)

    @pl.when(i + 1 < pl.num_programs(0))
    def _():
        issue(True, 1 - slot)

    return slot


def _gather_norm_body(idx_ref, nxt_ref, h_hbm, g_ref, o_ref, buf, sem):
    def issue(ahead, slot):
        _start_rows(h_hbm, nxt_ref if ahead else idx_ref, buf.at[slot], sem.at[slot])

    slot = _prefetch_schedule(issue)
    _wait_rows(h_hbm, buf.at[slot], sem.at[slot])
    o_ref[...] = _rms(buf[slot], g_ref[...]).astype(o_ref.dtype)


def _tile_index_specs(n_tiles, rows):
    return [pl.BlockSpec((1, 1, rows), lambda i: (i, 0, 0), memory_space=pltpu.SMEM),
            pl.BlockSpec((1, 1, rows), lambda i: (jnp.minimum(i + 1, n_tiles - 1), 0, 0), memory_space=pltpu.SMEM)]


def _gather_norm(h, g, row_token, rows=256):
    d = h.shape[1]
    total = row_token.shape[0]
    n_tiles = total // rows
    idx = row_token.reshape(n_tiles, 1, rows)
    return pl.pallas_call(
        _gather_norm_body,
        out_shape=jax.ShapeDtypeStruct((total, d), BF16),
        grid=(n_tiles,),
        in_specs=_tile_index_specs(n_tiles, rows) + [
            pl.BlockSpec(memory_space=pl.ANY),
            pl.BlockSpec((1, d), lambda i: (0, 0))],
        out_specs=pl.BlockSpec((rows, d), lambda i: (i, 0)),
        scratch_shapes=[pltpu.VMEM((2, rows, d), F32), pltpu.SemaphoreType.DMA((2,))],
        compiler_params=_cparams(("arbitrary",)),
        name="moe_gather",
    )(idx, idx, h, g.reshape(1, d))


def _group_first(te_ref, tv_ref, r):
    prev = te_ref[jnp.maximum(r - 1, 0)]
    return (r == 0) | (te_ref[r] != prev)


def _gffn1_body(te_ref, tv_ref, tx_ref, x_ref, w1_ref, w3_ref, o_ref, w1b, w3b):
    r = pl.program_id(1)

    @pl.when(_group_first(te_ref, tv_ref, r))
    def _():
        w1b[...] = w1_ref[0].astype(BF16)
        w3b[...] = w3_ref[0].astype(BF16)

    @pl.when(tv_ref[r] == 1)
    def _():
        x = x_ref[...]
        a = jnp.dot(x, w1b[...], preferred_element_type=F32)
        b = jnp.dot(x, w3b[...], preferred_element_type=F32)
        o_ref[...] = (_silu(a) * b).astype(o_ref.dtype)

    @pl.when(tv_ref[r] == 0)
    def _():
        o_ref[...] = jnp.zeros_like(o_ref)


def _gmm_body(te_ref, tv_ref, tx_ref, x_ref, w_ref, o_ref, wb):
    r = pl.program_id(1)

    @pl.when(_group_first(te_ref, tv_ref, r))
    def _():
        wb[...] = w_ref[0].astype(BF16)

    @pl.when(tv_ref[r] == 1)
    def _():
        o_ref[...] = jnp.dot(x_ref[...], wb[...], preferred_element_type=F32).astype(o_ref.dtype)

    @pl.when(tv_ref[r] == 0)
    def _():
        o_ref[...] = jnp.zeros_like(o_ref)


def _grouped(body, name, x, ws, tables, tr, tn, out_dtype):
    total, k = x.shape
    n = ws[0].shape[2]
    tn = _tile(n, tn)
    n_tiles = total // tr
    wspec = pl.BlockSpec((1, k, tn), lambda j, r, te, tv, tx: (te[r], 0, j))
    return pl.pallas_call(
        body,
        out_shape=jax.ShapeDtypeStruct((total, n), out_dtype),
        grid_spec=pltpu.PrefetchScalarGridSpec(
            num_scalar_prefetch=3,
            grid=(n // tn, n_tiles),
            in_specs=[pl.BlockSpec((tr, k), lambda j, r, te, tv, tx: (tx[r], 0))] + [wspec] * len(ws),
            out_specs=pl.BlockSpec((tr, tn), lambda j, r, te, tv, tx: (r, j)),
            scratch_shapes=[pltpu.VMEM((k, tn), BF16)] * len(ws),
        ),
        compiler_params=_cparams(("arbitrary", "arbitrary"), VMEM_LIMIT),
        name=name,
    )(*tables, x, *ws)


def _combine_body(d0_ref, n0_ref, d1_ref, n1_ref, y_hbm, h_ref, info_ref, g_ref, o_ref, buf0, buf1, sem0, sem1):
    def issue(ahead, slot):
        _start_rows(y_hbm, n0_ref if ahead else d0_ref, buf0.at[slot], sem0.at[slot])
        _start_rows(y_hbm, n1_ref if ahead else d1_ref, buf1.at[slot], sem1.at[slot])

    slot = _prefetch_schedule(issue)
    _wait_rows(y_hbm, buf0.at[slot], sem0.at[slot])
    _wait_rows(y_hbm, buf1.at[slot], sem1.at[slot])
    info = info_ref[...]
    lane = lax.broadcasted_iota(jnp.int32, info.shape, 1)
    g0 = jnp.sum(jnp.where(lane == 2, info, 0.0), axis=1, keepdims=True)
    g1 = jnp.sum(jnp.where(lane == 3, info, 0.0), axis=1, keepdims=True)
    out = h_ref[...] + g0 * buf0[slot] + g1 * buf1[slot]
    o_ref[...] = _rms(out, g_ref[...])


def _combine_norm(h, y, dest, info, g, rows=256):
    t, d = h.shape
    rows = min(rows, t)
    n_tiles = t // rows
    d0 = dest[:, 0].reshape(n_tiles, 1, rows)
    d1 = dest[:, 1].reshape(n_tiles, 1, rows)
    return pl.pallas_call(
        _combine_body,
        out_shape=jax.ShapeDtypeStruct((t, d), F32),
        grid=(n_tiles,),
        in_specs=_tile_index_specs(n_tiles, rows) + _tile_index_specs(n_tiles, rows) + [
            pl.BlockSpec(memory_space=pl.ANY),
            pl.BlockSpec((rows, d), lambda i: (i, 0)),
            pl.BlockSpec((rows, LANES), lambda i: (i, 0)),
            pl.BlockSpec((1, d), lambda i: (0, 0))],
        out_specs=pl.BlockSpec((rows, d), lambda i: (i, 0)),
        scratch_shapes=[pltpu.VMEM((2, rows, d), F32), pltpu.VMEM((2, rows, d), F32),
                        pltpu.SemaphoreType.DMA((2,)), pltpu.SemaphoreType.DMA((2,))],
        compiler_params=_cparams(("arbitrary",)),
        name="moe_combine",
    )(d0, d0, d1, d1, y, h, info, g.reshape(1, d))


def _moe_final(h, ln_g, router_w, w1, w3, w2, final_g, tr=512, tn_up=512, tn_down=1024):
    t, d = h.shape
    n_exp = router_w.shape[1]
    tr = min(tr, t)
    info, cnt = _router(h, ln_g, router_w)

    expert = info[:, 0:TOP_K].astype(jnp.int32)
    rank = info[:, 4:4 + TOP_K].astype(jnp.int32)
    counts = cnt[0, :n_exp].astype(jnp.int32)
    tiles_per = (counts + tr - 1) // tr
    tile_end = jnp.cumsum(tiles_per)
    tile_start = tile_end - tiles_per
    n_tiles = TOP_K * t // tr + n_exp
    total = n_tiles * tr
    dest = tile_start[expert] * tr + rank
    token = jnp.broadcast_to(jnp.arange(t, dtype=jnp.int32)[:, None], (t, TOP_K))
    row_token = jnp.zeros((total,), jnp.int32).at[dest.reshape(-1)].set(token.reshape(-1))
    tile_id = jnp.arange(n_tiles, dtype=jnp.int32)
    n_valid = tile_end[n_exp - 1]
    last = jnp.maximum(n_valid - 1, 0)
    tile_valid = (tile_id < n_valid).astype(jnp.int32)
    tile_src = jnp.minimum(tile_id, last)
    tile_expert = jnp.minimum(jnp.searchsorted(tile_end, tile_src, side="right"), n_exp - 1).astype(jnp.int32)
    tables = (tile_expert, tile_valid, tile_src)

    xs = _gather_norm(h, ln_g, row_token)
    gs = _grouped(_gffn1_body, "moe_up", xs, (w1, w3), tables, tr, tn_up, BF16)
    ys = _grouped(_gmm_body, "moe_down", gs, (w2,), tables, tr, tn_down, F32)
    return _combine_norm(h, ys, dest, info, final_g)


def kernel(x, ln1_g, w_in, b_f, w_s, b_s, conv_w, lam_q1, lam_k1, lam_q2, lam_k2, subln_g, w_o, ln2_g,
           ffn_w1, ffn_w3, ffn_w2, router_w, moe_w1, moe_w3, moe_w2, final_g):
    bsz, seq, d = x.shape
    depth = ln1_g.shape[0]
    nh = b_f.shape[1]
    dg = nh * HEAD_DIM
    t = bsz * seq
    assert w_in.shape[2] == 11 * dg + nh and d == 4 * dg and depth % 2 == 0
    f0 = 5 * dg
    blk = lambda cols: cols // HEAD_DIM
    col_fq, col_fk, col_fv = blk(2 * dg), blk(3 * dg), blk(4 * dg)
    col_dq, col_dk, col_dv = blk(8 * dg), blk(9 * dg), blk(10 * dg)

    h = x.reshape(t, d)
    out = None
    for i in range(depth):
        w_main = jnp.concatenate([w_in[i, :, :f0], w_in[i, :, f0 + nh:]], axis=1).astype(BF16)
        w_f = jnp.zeros((d, LANES), BF16).at[:, :nh].set(w_in[i, :, f0:f0 + nh].astype(BF16))

        n, flog = _ln1(h, ln1_g[i], w_f)
        proj = _matmul(n, w_main, BF16, tm=1024, tn=1024)
        proj3 = proj.reshape(bsz, seq, proj.shape[1])

        out_a = _sgu(proj, w_s[i], b_s[i], dg)
        c_col = _logsig_cumsum(flog.reshape(bsz, seq, LANES), b_f[i])
        c_row = jnp.transpose(c_col[:, :, :nh], (0, 2, 1))
        out_b = _fox(proj3, c_col, c_row, nh, col_fq, col_fk, col_fv).reshape(t, dg)
        out_c = _conv(proj, conv_w[i], seq, dg, 5 * dg)
        lam_init = 0.8 - 0.6 * math.exp(-0.3 * i)
        lam_vecs = jnp.stack([lam_q1[i], lam_k1[i], lam_q2[i], lam_k2[i]]).astype(F32)
        out_d = _diff(proj3, lam_vecs, subln_g[i], nh, col_dq, col_dk, col_dv, lam_init).reshape(t, dg)

        h = _out_proj((out_a, out_b, out_c, out_d), w_o[i], h, tm=1024, tn=512)

        j = i // 2
        if i % 2 == 0:
            n2 = _rmsnorm(h, ln2_g[i], BF16)
            g = _ffn1(n2, ffn_w1[j], ffn_w3[j], tm=1024, tn=256)
            h = _ffn2(g, ffn_w2[j], h, tm=1024, tn=256)
        else:
            assert i == depth - 1
            out = _moe_final(h, ln2_g[i], router_w[j], moe_w1[j], moe_w3[j], moe_w2[j], final_g)
    return out.reshape(bsz, seq, d)
```

```python
import functools
import math

import jax
import jax.numpy as jnp
from jax import lax
from jax.experimental import pallas as pl
from jax.experimental.pallas import tpu as pltpu

F32 = jnp.float32
BF16 = jnp.bfloat16

HEAD_DIM = 128
CHUNK = 64
SGU_BLOCK = 128
CONV_WIDTH = 3
TOP_K = 2
RMS_EPS = 1e-6
NEG_INF = -1e30
LOG2E = 1.4426950408889634
LANES = 128
BF16_ROWS = 16
ROW_DMA_UNROLL = 8
VMEM_LIMIT = 58 * 1024 * 1024


def _cparams(sem, vmem=None):
    return pltpu.CompilerParams(dimension_semantics=sem, vmem_limit_bytes=vmem)


def _tile(dim, target, unit=LANES):
    if dim <= target:
        return dim
    best = max(c for c in range(unit, target + 1, unit) if dim % c == 0)
    return best


def _rms(x, g):
    ms = jnp.mean(x * x, axis=-1, keepdims=True)
    return x * lax.rsqrt(ms + RMS_EPS) * g


def _ln1_body(x_ref, g_ref, wf_ref, n_ref, f_ref):
    nb = _rms(x_ref[...], g_ref[...]).astype(BF16)
    n_ref[...] = nb
    f_ref[...] = jnp.dot(nb, wf_ref[...], preferred_element_type=F32)


def _ln1(h, g, wf, tm=256):
    t, d = h.shape
    tm = min(tm, t)
    return pl.pallas_call(
        _ln1_body,
        out_shape=(jax.ShapeDtypeStruct((t, d), BF16), jax.ShapeDtypeStruct((t, LANES), F32)),
        grid=(t // tm,),
        in_specs=[pl.BlockSpec((tm, d), lambda i: (i, 0)),
                  pl.BlockSpec((1, d), lambda i: (0, 0)),
                  pl.BlockSpec((d, LANES), lambda i: (0, 0))],
        out_specs=(pl.BlockSpec((tm, d), lambda i: (i, 0)),
                   pl.BlockSpec((tm, LANES), lambda i: (i, 0))),
        compiler_params=_cparams(("parallel",)),
        name="ln1_forget",
    )(h, g.reshape(1, d), wf)


def _rmsnorm_body(x_ref, g_ref, o_ref):
    o_ref[...] = _rms(x_ref[...], g_ref[...]).astype(o_ref.dtype)


def _rmsnorm(h, g, out_dtype, tm=256):
    t, d = h.shape
    tm = min(tm, t)
    return pl.pallas_call(
        _rmsnorm_body,
        out_shape=jax.ShapeDtypeStruct((t, d), out_dtype),
        grid=(t // tm,),
        in_specs=[pl.BlockSpec((tm, d), lambda i: (i, 0)),
                  pl.BlockSpec((1, d), lambda i: (0, 0))],
        out_specs=pl.BlockSpec((tm, d), lambda i: (i, 0)),
        compiler_params=_cparams(("parallel",)),
        name="rmsnorm",
    )(h, g.reshape(1, d))


def _mm_body(x_ref, w_ref, o_ref):
    o_ref[...] = jnp.dot(x_ref[...], w_ref[...].astype(BF16),
                         preferred_element_type=F32).astype(o_ref.dtype)


def _matmul(x, w, layer, n, out_dtype, tm, tn):
    t, k = x.shape
    tm, tn = _tile(t, tm, 8), _tile(n, tn)
    return pl.pallas_call(
        _mm_body,
        out_shape=jax.ShapeDtypeStruct((t, n), out_dtype),
        grid=(t // tm, n // tn),
        in_specs=[pl.BlockSpec((tm, k), lambda i, j: (i, 0)),
                  pl.BlockSpec((None, k, tn), lambda i, j: (layer, 0, j))],
        out_specs=pl.BlockSpec((tm, tn), lambda i, j: (i, j)),
        compiler_params=_cparams(("parallel", "arbitrary"), VMEM_LIMIT),
        name="in_proj",
    )(x, w)


def _oproj_body(a_ref, b_ref, c_ref, d_ref, w_ref, r_ref, o_ref, *, dg):
    acc = r_ref[...]
    for part, x_ref in enumerate((a_ref, b_ref, c_ref, d_ref)):
        w = w_ref[part * dg:(part + 1) * dg, :].astype(BF16)
        acc = acc + jnp.dot(x_ref[...], w, preferred_element_type=F32)
    o_ref[...] = acc


def _out_proj(parts, w, layer, res, tm, tn):
    t, dg = parts[0].shape
    _, k, n = w.shape
    tm, tn = _tile(t, tm, 8), _tile(n, tn)
    xspec = pl.BlockSpec((tm, dg), lambda i, j: (i, 0))
    return pl.pallas_call(
        functools.partial(_oproj_body, dg=dg),
        out_shape=jax.ShapeDtypeStruct((t, n), F32),
        grid=(t // tm, n // tn),
        in_specs=[xspec, xspec, xspec, xspec,
                  pl.BlockSpec((None, k, tn), lambda i, j: (layer, 0, j)),
                  pl.BlockSpec((tm, tn), lambda i, j: (i, j))],
        out_specs=pl.BlockSpec((tm, tn), lambda i, j: (i, j)),
        compiler_params=_cparams(("parallel", "arbitrary"), VMEM_LIMIT),
        name="out_proj",
    )(*parts, w, res)


def _silu(x):
    return x * (1.0 / (1.0 + jnp.exp(-x)))


def _ffn1_body(x_ref, w1_ref, w3_ref, o_ref):
    x = x_ref[...]
    a = jnp.dot(x, w1_ref[...].astype(BF16), preferred_element_type=F32)
    b = jnp.dot(x, w3_ref[...].astype(BF16), preferred_element_type=F32)
    o_ref[...] = (_silu(a) * b).astype(o_ref.dtype)


def _ffn1(x, w1, w3, tm, tn):
    t, k = x.shape
    n = w1.shape[1]
    tm, tn = _tile(t, tm, 8), _tile(n, tn)
    wspec = pl.BlockSpec((k, tn), lambda i, j: (0, j))
    return pl.pallas_call(
        _ffn1_body,
        out_shape=jax.ShapeDtypeStruct((t, n), BF16),
        grid=(t // tm, n // tn),
        in_specs=[pl.BlockSpec((tm, k), lambda i, j: (i, 0)), wspec, wspec],
        out_specs=pl.BlockSpec((tm, tn), lambda i, j: (i, j)),
        compiler_params=_cparams(("parallel", "arbitrary"), VMEM_LIMIT),
        name="ffn_up",
    )(x, w1, w3)


def _mm_res_body(x_ref, w_ref, r_ref, o_ref):
    o_ref[...] = r_ref[...] + jnp.dot(x_ref[...], w_ref[...].astype(BF16), preferred_element_type=F32)


def _ffn2(x, w, res, tm, tn):
    t, k = x.shape
    n = w.shape[1]
    tm, tn = _tile(t, tm, 8), _tile(n, tn)
    return pl.pallas_call(
        _mm_res_body,
        out_shape=jax.ShapeDtypeStruct((t, n), F32),
        grid=(t // tm, n // tn),
        in_specs=[pl.BlockSpec((tm, k), lambda i, j: (i, 0), pipeline_mode=pl.Buffered(1)),
                  pl.BlockSpec((k, tn), lambda i, j: (0, j)),
                  pl.BlockSpec((tm, tn), lambda i, j: (i, j))],
        out_specs=pl.BlockSpec((tm, tn), lambda i, j: (i, j)),
        compiler_params=_cparams(("parallel", "arbitrary"), VMEM_LIMIT),
        name="ffn_down",
    )(x, w, res)


def _gelu_tanh(x):
    return 0.5 * x * (1.0 + jnp.tanh(math.sqrt(2.0 / math.pi) * (x + 0.044715 * (x * x * x))))


def _sgu_body(z_ref, w_ref, b_ref, o_ref, *, dg, nh):
    rows = z_ref.shape[0]
    z = _gelu_tanh(z_ref[...].astype(F32))
    u = z[:, :dg]
    v = z[:, dg:]
    mu = jnp.mean(v, axis=-1, keepdims=True)
    vc = v - mu
    var = jnp.mean(vc * vc, axis=-1, keepdims=True)
    vn = (vc * lax.rsqrt(var + RMS_EPS)).astype(BF16)
    t_chunk = lax.broadcasted_iota(jnp.int32, (SGU_BLOCK, SGU_BLOCK), 0) // CHUNK
    s_chunk = lax.broadcasted_iota(jnp.int32, (SGU_BLOCK, SGU_BLOCK), 1) // CHUNK
    allowed = t_chunk >= s_chunk
    for hd in range(nh):
        wm = jnp.where(allowed, w_ref[hd], 0.0).astype(BF16)
        bias = b_ref[hd]
        cs = slice(hd * HEAD_DIM, (hd + 1) * HEAD_DIM)
        for blk in range(rows // SGU_BLOCK):
            rs = slice(blk * SGU_BLOCK, (blk + 1) * SGU_BLOCK)
            mixed = jnp.dot(wm, vn[rs, cs], preferred_element_type=F32) + bias
            o_ref[rs, cs] = (u[rs, cs] * mixed).astype(o_ref.dtype)


def _sgu(proj, w_s, b_s, dg, rows=256):
    t = proj.shape[0]
    nh = w_s.shape[0]
    rows = min(rows, t)
    bias = jnp.broadcast_to(b_s[:, :, None], (nh, SGU_BLOCK, HEAD_DIM))
    return pl.pallas_call(
        functools.partial(_sgu_body, dg=dg, nh=nh),
        out_shape=jax.ShapeDtypeStruct((t, dg), BF16),
        grid=(t // rows,),
        in_specs=[pl.BlockSpec((rows, 2 * dg), lambda i: (i, 0)),
                  pl.BlockSpec((nh, SGU_BLOCK, SGU_BLOCK), lambda i: (0, 0, 0)),
                  pl.BlockSpec((nh, SGU_BLOCK, HEAD_DIM), lambda i: (0, 0, 0))],
        out_specs=pl.BlockSpec((rows, dg), lambda i: (i, 0)),
        compiler_params=_cparams(("parallel",)),
        name="sgu",
    )(proj, w_s, bias)


def _conv_body(x_ref, b_ref, c_ref, xh_ref, ch_ref, w_ref, o_ref, *, tiles_per_seq):
    i = pl.program_id(0)
    rows = x_ref.shape[0]
    hcur = c_ref[...].astype(F32) * x_ref[...].astype(F32)
    halo = ch_ref[...].astype(F32) * xh_ref[...].astype(F32)
    halo = jnp.where(i % tiles_per_seq == 0, 0.0, halo)
    hp = jnp.concatenate([halo, hcur], axis=0)
    w = w_ref[...]
    y = w[CONV_WIDTH - 1:CONV_WIDTH, :] * hcur
    for tap in range(CONV_WIDTH - 1):
        off = BF16_ROWS - (CONV_WIDTH - 1) + tap
        y = y + w[tap:tap + 1, :] * hp[off:off + rows, :]
    o_ref[...] = (b_ref[...].astype(F32) * y).astype(o_ref.dtype)


def _conv(proj, conv_w, seq, dg, col0, rows=512):
    t = proj.shape[0]
    rows = min(rows, seq)
    cb = col0 // dg
    hb = rows // BF16_ROWS
    cur = lambda off: pl.BlockSpec((rows, dg), lambda i: (i, cb + off))
    halo = lambda off: pl.BlockSpec((BF16_ROWS, dg), lambda i: (jnp.maximum(i * hb - 1, 0), cb + off))
    wpad = jnp.zeros((8, dg), F32).at[:CONV_WIDTH].set(conv_w)
    return pl.pallas_call(
        functools.partial(_conv_body, tiles_per_seq=seq // rows),
        out_shape=jax.ShapeDtypeStruct((t, dg), BF16),
        grid=(t // rows,),
        in_specs=[cur(0), cur(1), cur(2), halo(0), halo(2),
                  pl.BlockSpec((8, dg), lambda i: (0, 0))],
        out_specs=pl.BlockSpec((rows, dg), lambda i: (i, 0)),
        compiler_params=_cparams(("parallel",)),
        name="short_conv",
    )(proj, proj, proj, proj, proj, wpad)


def _logsig_cumsum_body(f_ref, b_ref, c_ref, carry_ref):
    s = pl.program_id(1)

    @pl.when(s == 0)
    def _():
        carry_ref[...] = jnp.zeros_like(carry_ref)

    x = f_ref[0] + b_ref[...]
    ls = jnp.minimum(x, 0.0) - jnp.log(1.0 + jnp.exp(-jnp.abs(x)))
    rows = ls.shape[0]
    row = lax.broadcasted_iota(jnp.int32, ls.shape, 0)
    shift = 1
    while shift < rows:
        ls = ls + jnp.where(row >= shift, pltpu.roll(ls, shift, 0), 0.0)
        shift *= 2
    out = ls + carry_ref[0:1, :]
    c_ref[0] = out
    carry_ref[...] = jnp.broadcast_to(out[rows - 1:rows, :], carry_ref.shape)


def _logsig_cumsum(f, b_f, rows=512):
    bsz, seq, _ = f.shape
    rows = min(rows, seq)
    bias = jnp.zeros((1, LANES), F32).at[0, :b_f.shape[0]].set(b_f)
    return pl.pallas_call(
        _logsig_cumsum_body,
        out_shape=jax.ShapeDtypeStruct(f.shape, F32),
        grid=(bsz, seq // rows),
        in_specs=[pl.BlockSpec((1, rows, LANES), lambda b, s: (b, s, 0)),
                  pl.BlockSpec((1, LANES), lambda b, s: (0, 0))],
        out_specs=pl.BlockSpec((1, rows, LANES), lambda b, s: (b, s, 0)),
        scratch_shapes=[pltpu.VMEM((8, LANES), F32)],
        compiler_params=_cparams(("parallel", "arbitrary")),
        name="forget_cumsum",
    )(f, bias)


def _qk(q, k):
    return lax.dot_general(q, k, (((1,), (1,)), ((), ())), preferred_element_type=F32)


def _online_softmax(t, row_term, v, carry):
    m, l, acc = carry
    m_new = jnp.maximum(m, jnp.max(t, axis=1, keepdims=True) + row_term)
    p = jnp.exp2(t - (m_new - row_term))
    alpha = jnp.exp2(m - m_new)
    l = alpha * l + jnp.sum(p, axis=1, keepdims=True)
    acc = alpha * acc + jnp.dot(p.astype(BF16), v, preferred_element_type=F32)
    return m_new, l, acc


def _below_diagonal(step, n_blocks, carry):
    def pair(p, c):
        return step(2 * p + 1, step(2 * p, c, False), False)

    carry = lax.fori_loop(0, n_blocks // 2, pair, carry)
    return lax.fori_loop(2 * (n_blocks // 2), n_blocks, lambda j, c: step(j, c, False), carry)


def _softmax_init(rows):
    return (jnp.full((rows, 1), NEG_INF, F32), jnp.zeros((rows, 1), F32), jnp.zeros((rows, HEAD_DIM), F32))


def _fox_body(q_ref, k_ref, v_ref, ccol_ref, crow_ref, o_ref, *, tq, scale):
    hd = pl.program_id(1)
    qi = pl.program_id(2)
    q = q_ref[0]
    lane = lax.broadcasted_iota(jnp.int32, (tq, LANES), 1)
    cq = jnp.sum(jnp.where(lane == hd, ccol_ref[0], 0.0), axis=1, keepdims=True) * LOG2E

    def step(j, carry, diagonal):
        start = pl.multiple_of(j * tq, tq)
        k = k_ref[0, pl.ds(start, tq), :]
        v = v_ref[0, pl.ds(start, tq), :]
        t = _qk(q, k) * (scale * LOG2E) - crow_ref[0, 0, pl.ds(j, 1), :] * LOG2E
        if diagonal:
            r = lax.broadcasted_iota(jnp.int32, (tq, tq), 0)
            c = lax.broadcasted_iota(jnp.int32, (tq, tq), 1)
            t = jnp.where(c <= r, t, NEG_INF)
        return _online_softmax(t, cq, v, carry)

    _, l, acc = step(qi, _below_diagonal(step, qi, _softmax_init(tq)), True)
    o_ref[0] = (acc / l).astype(o_ref.dtype)


def _fox(proj3, c_col, c_row, nh, col_q, col_k, col_v, tq=512):
    bsz, seq, _ = proj3.shape
    tq = min(tq, seq)
    nq = seq // tq
    c_row = c_row.reshape(bsz, nh, nq, tq)
    kv = lambda col: pl.BlockSpec((1, seq, HEAD_DIM), lambda b, h, i: (b, 0, col + h))
    return pl.pallas_call(
        functools.partial(_fox_body, tq=tq, scale=HEAD_DIM ** -0.5),
        out_shape=jax.ShapeDtypeStruct((bsz, seq, nh * HEAD_DIM), BF16),
        grid=(bsz, nh, nq),
        in_specs=[pl.BlockSpec((1, tq, HEAD_DIM), lambda b, h, i: (b, i, col_q + h)),
                  kv(col_k), kv(col_v),
                  pl.BlockSpec((1, tq, LANES), lambda b, h, i: (b, i, 0)),
                  pl.BlockSpec((1, 1, nq, tq), lambda b, h, i: (b, h, 0, 0))],
        out_specs=pl.BlockSpec((1, tq, HEAD_DIM), lambda b, h, i: (b, i, h)),
        compiler_params=_cparams(("parallel", "parallel", "arbitrary")),
        name="fox_attention",
    )(proj3, proj3, proj3, c_col, c_row)


def _diff_body(slope_ref, q_ref, k_ref, v_ref, lam_ref, g_ref, o_ref, *, tq, scale, lam_init):
    hd = pl.program_id(1)
    qi = pl.program_id(2)
    half = HEAD_DIM // 2
    slope = slope_ref[hd] * LOG2E
    lv = lam_ref[...]
    lam = (jnp.exp(jnp.sum(lv[0:1] * lv[1:2], axis=1, keepdims=True))
           - jnp.exp(jnp.sum(lv[2:3] * lv[3:4], axis=1, keepdims=True)) + lam_init)
    q = q_ref[0]
    first = lax.broadcasted_iota(jnp.int32, (tq, HEAD_DIM), 1) < half
    zero = jnp.zeros_like(q)
    q2 = jnp.concatenate([jnp.where(first, q, zero), jnp.where(first, zero, q)], axis=0)
    row = lax.broadcasted_iota(jnp.int32, (2 * tq, 1), 0)
    row = jnp.where(row >= tq, row - tq, row)
    col_bias = slope * lax.broadcasted_iota(jnp.int32, (1, tq), 1).astype(F32)

    def step(j, carry, diagonal):
        start = pl.multiple_of(j * tq, tq)
        k = k_ref[0, pl.ds(start, tq), :]
        v = v_ref[0, pl.ds(start, tq), :]
        t = _qk(q2, k) * (scale * LOG2E)
        if diagonal:
            c = lax.broadcasted_iota(jnp.int32, (2 * tq, tq), 1)
            t = jnp.where((c // CHUNK) <= (row // CHUNK), t - slope * jnp.abs(row - c).astype(F32), NEG_INF)
            row_term = jnp.zeros((2 * tq, 1), F32)
        else:
            t = t + col_bias
            row_term = -slope * (row + (qi - j) * tq).astype(F32)
        return _online_softmax(t, row_term, v, carry)

    _, l, acc = step(qi, _below_diagonal(step, qi, _softmax_init(2 * tq)), True)
    a = acc / l
    out = a[:tq] - lam * a[tq:]
    o_ref[0] = (_rms(out, g_ref[...]) * (1.0 - lam_init)).astype(o_ref.dtype)


def _diff(proj3, lam_vecs, subln_g, nh, col_q, col_k, col_v, lam_init, tq=512):
    bsz, seq, _ = proj3.shape
    tq = min(tq, seq)
    nq = seq // tq
    slopes = (2.0 ** (-8.0 * jnp.arange(1, nh + 1, dtype=F32) / nh)).astype(F32)
    kv = lambda col: pl.BlockSpec((1, seq, HEAD_DIM), lambda b, h, i, s: (b, 0, col + h))
    return pl.pallas_call(
        functools.partial(_diff_body, tq=tq, scale=(HEAD_DIM // 2) ** -0.5, lam_init=lam_init),
        out_shape=jax.ShapeDtypeStruct((bsz, seq, nh * HEAD_DIM), BF16),
        grid_spec=pltpu.PrefetchScalarGridSpec(
            num_scalar_prefetch=1,
            grid=(bsz, nh, nq),
            in_specs=[pl.BlockSpec((1, tq, HEAD_DIM), lambda b, h, i, s: (b, i, col_q + h)),
                      kv(col_k), kv(col_v),
                      pl.BlockSpec(lam_vecs.shape, lambda b, h, i, s: (0, 0)),
                      pl.BlockSpec((1, HEAD_DIM), lambda b, h, i, s: (0, 0))],
            out_specs=pl.BlockSpec((1, tq, HEAD_DIM), lambda b, h, i, s: (b, i, h)),
        ),
        compiler_params=_cparams(("parallel", "parallel", "arbitrary")),
        name="diff_attention",
    )(slopes, proj3, proj3, proj3, lam_vecs, subln_g.reshape(1, HEAD_DIM))


def _router_body(h_ref, g_ref, rw_ref, info_ref, cnt_ref, carry_ref, *, n_exp):
    i = pl.program_id(0)

    @pl.when(i == 0)
    def _():
        carry_ref[...] = jnp.zeros_like(carry_ref)

    tm = h_ref.shape[0]
    n = _rms(h_ref[...], g_ref[...])
    logits = jnp.dot(n, rw_ref[...], preferred_element_type=F32, precision=lax.Precision.HIGHEST)
    lane = lax.broadcasted_iota(jnp.int32, (tm, LANES), 1).astype(F32)
    logits = jnp.where(lane < n_exp, logits, -jnp.inf)
    v1 = jnp.max(logits, axis=1, keepdims=True)
    i1 = jnp.min(jnp.where(logits == v1, lane, float(LANES)), axis=1, keepdims=True)
    rest = jnp.where(lane == i1, -jnp.inf, logits)
    v2 = jnp.max(rest, axis=1, keepdims=True)
    i2 = jnp.min(jnp.where(rest == v2, lane, float(LANES)), axis=1, keepdims=True)
    ratio = jnp.exp(v2 - v1)
    g1 = 1.0 / (1.0 + ratio)
    g2 = ratio / (1.0 + ratio)
    hit1 = lane == i1
    hit2 = lane == i2
    onehot = jnp.where(hit1 | hit2, 1.0, 0.0)
    r = lax.broadcasted_iota(jnp.int32, (tm, tm), 0)
    c = lax.broadcasted_iota(jnp.int32, (tm, tm), 1)
    before = jnp.where(c < r, 1.0, 0.0).astype(BF16)
    rank = jnp.dot(before, onehot.astype(BF16), preferred_element_type=F32) + carry_ref[0:1, :]
    rank1 = jnp.sum(jnp.where(hit1, rank, 0.0), axis=1, keepdims=True)
    rank2 = jnp.sum(jnp.where(hit2, rank, 0.0), axis=1, keepdims=True)
    total = carry_ref[0:1, :] + jnp.sum(onehot, axis=0, keepdims=True)
    carry_ref[...] = jnp.broadcast_to(total, carry_ref.shape)
    cnt_ref[...] = jnp.broadcast_to(total, cnt_ref.shape)
    fields = (i1, i2, g1, g2, rank1, rank2)
    info = jnp.zeros((tm, LANES), F32)
    for pos, val in enumerate(fields):
        info = jnp.where(lane == pos, val, info)
    info_ref[...] = info


def _router(h, g, router_w, tm=256):
    t, d = h.shape
    n_exp = router_w.shape[1]
    tm = min(tm, t)
    rw = jnp.zeros((d, LANES), F32).at[:, :n_exp].set(router_w)
    return pl.pallas_call(
        functools.partial(_router_body, n_exp=n_exp),
        out_shape=(jax.ShapeDtypeStruct((t, LANES), F32), jax.ShapeDtypeStruct((8, LANES), F32)),
        grid=(t // tm,),
        in_specs=[pl.BlockSpec((tm, d), lambda i: (i, 0)),
                  pl.BlockSpec((1, d), lambda i: (0, 0)),
                  pl.BlockSpec((d, LANES), lambda i: (0, 0))],
        out_specs=(pl.BlockSpec((tm, LANES), lambda i: (i, 0)),
                   pl.BlockSpec((8, LANES), lambda i: (0, 0))),
        scratch_shapes=[pltpu.VMEM((8, LANES), F32)],
        compiler_params=_cparams(("arbitrary",)),
        name="router",
    )(h, g.reshape(1, d), rw)


def _row_copy(src_hbm, dst, sem, src_row, dst_row):
    return pltpu.make_async_copy(src_hbm.at[pl.ds(src_row, 1)], dst.at[pl.ds(dst_row, 1)], sem)


def _start_rows(src_hbm, idx_ref, dst, sem):
    def body(r, c):
        _row_copy(src_hbm, dst, sem, idx_ref[0, 0, r], r).start()
        return c

    lax.fori_loop(0, dst.shape[0], body, 0, unroll=ROW_DMA_UNROLL)


def _wait_rows(src_hbm, dst, sem):
    def body(r, c):
        _row_copy(src_hbm, dst, sem, 0, r).wait()
        return c

    lax.fori_loop(0, dst.shape[0], body, 0, unroll=ROW_DMA_UNROLL)


def _prefetch_schedule(issue):
    i = pl.program_id(0)
    slot = i % 2

    @pl.when(i == 0)
    def _():
        issue(False, 0)

    @pl.when(i + 1 < pl.num_programs(0))
    def _():
        issue(True, 1 - slot)

    return slot


def _gather_norm_body(idx_ref, nxt_ref, h_hbm, g_ref, o_ref, buf, sem):
    def issue(ahead, slot):
        _start_rows(h_hbm, nxt_ref if ahead else idx_ref, buf.at[slot], sem.at[slot])

    slot = _prefetch_schedule(issue)
    _wait_rows(h_hbm, buf.at[slot], sem.at[slot])
    o_ref[...] = _rms(buf[slot], g_ref[...]).astype(o_ref.dtype)


def _tile_index_specs(n_tiles, rows):
    return [pl.BlockSpec((1, 1, rows), lambda i: (i, 0, 0), memory_space=pltpu.SMEM),
            pl.BlockSpec((1, 1, rows), lambda i: (jnp.minimum(i + 1, n_tiles - 1), 0, 0), memory_space=pltpu.SMEM)]


def _gather_norm(h, g, row_token, rows=256):
    d = h.shape[1]
    total = row_token.shape[0]
    n_tiles = total // rows
    idx = row_token.reshape(n_tiles, 1, rows)
    return pl.pallas_call(
        _gather_norm_body,
        out_shape=jax.ShapeDtypeStruct((total, d), BF16),
        grid=(n_tiles,),
        in_specs=_tile_index_specs(n_tiles, rows) + [
            pl.BlockSpec(memory_space=pl.ANY),
            pl.BlockSpec((1, d), lambda i: (0, 0))],
        out_specs=pl.BlockSpec((rows, d), lambda i: (i, 0)),
        scratch_shapes=[pltpu.VMEM((2, rows, d), F32), pltpu.SemaphoreType.DMA((2,))],
        compiler_params=_cparams(("arbitrary",)),
        name="moe_gather",
    )(idx, idx, h, g.reshape(1, d))


def _group_first(te_ref, tv_ref, r):
    prev = te_ref[jnp.maximum(r - 1, 0)]
    return (r == 0) | (te_ref[r] != prev)


def _gffn1_body(te_ref, tv_ref, tx_ref, x_ref, w1_ref, w3_ref, o_ref, w1b, w3b):
    r = pl.program_id(1)

    @pl.when(_group_first(te_ref, tv_ref, r))
    def _():
        w1b[...] = w1_ref[0].astype(BF16)
        w3b[...] = w3_ref[0].astype(BF16)

    @pl.when(tv_ref[r] == 1)
    def _():
        x = x_ref[...]
        a = jnp.dot(x, w1b[...], preferred_element_type=F32)
        b = jnp.dot(x, w3b[...], preferred_element_type=F32)
        o_ref[...] = (_silu(a) * b).astype(o_ref.dtype)

    @pl.when(tv_ref[r] == 0)
    def _():
        o_ref[...] = jnp.zeros_like(o_ref)


def _gmm_body(te_ref, tv_ref, tx_ref, x_ref, w_ref, o_ref, wb):
    r = pl.program_id(1)

    @pl.when(_group_first(te_ref, tv_ref, r))
    def _():
        wb[...] = w_ref[0].astype(BF16)

    @pl.when(tv_ref[r] == 1)
    def _():
        o_ref[...] = jnp.dot(x_ref[...], wb[...], preferred_element_type=F32).astype(o_ref.dtype)

    @pl.when(tv_ref[r] == 0)
    def _():
        o_ref[...] = jnp.zeros_like(o_ref)


def _grouped(body, name, x, ws, tables, tr, tn, out_dtype):
    total, k = x.shape
    n = ws[0].shape[2]
    tn = _tile(n, tn)
    n_tiles = total // tr
    wspec = pl.BlockSpec((1, k, tn), lambda j, r, te, tv, tx: (te[r], 0, j))
    return pl.pallas_call(
        body,
        out_shape=jax.ShapeDtypeStruct((total, n), out_dtype),
        grid_spec=pltpu.PrefetchScalarGridSpec(
            num_scalar_prefetch=3,
            grid=(n // tn, n_tiles),
            in_specs=[pl.BlockSpec((tr, k), lambda j, r, te, tv, tx: (tx[r], 0))] + [wspec] * len(ws),
            out_specs=pl.BlockSpec((tr, tn), lambda j, r, te, tv, tx: (r, j)),
            scratch_shapes=[pltpu.VMEM((k, tn), BF16)] * len(ws),
        ),
        compiler_params=_cparams(("arbitrary", "arbitrary"), VMEM_LIMIT),
        name=name,
    )(*tables, x, *ws)


def _combine_body(d0_ref, n0_ref, d1_ref, n1_ref, y_hbm, h_ref, info_ref, g_ref, o_ref, buf0, buf1, sem0, sem1):
    def issue(ahead, slot):
        _start_rows(y_hbm, n0_ref if ahead else d0_ref, buf0.at[slot], sem0.at[slot])
        _start_rows(y_hbm, n1_ref if ahead else d1_ref, buf1.at[slot], sem1.at[slot])

    slot = _prefetch_schedule(issue)
    _wait_rows(y_hbm, buf0.at[slot], sem0.at[slot])
    _wait_rows(y_hbm, buf1.at[slot], sem1.at[slot])
    info = info_ref[...]
    lane = lax.broadcasted_iota(jnp.int32, info.shape, 1)
    g0 = jnp.sum(jnp.where(lane == 2, info, 0.0), axis=1, keepdims=True)
    g1 = jnp.sum(jnp.where(lane == 3, info, 0.0), axis=1, keepdims=True)
    out = h_ref[...] + g0 * buf0[slot] + g1 * buf1[slot]
    o_ref[...] = _rms(out, g_ref[...])


def _combine_norm(h, y, dest, info, g, rows=256):
    t, d = h.shape
    rows = min(rows, t)
    n_tiles = t // rows
    d0 = dest[:, 0].reshape(n_tiles, 1, rows)
    d1 = dest[:, 1].reshape(n_tiles, 1, rows)
    return pl.pallas_call(
        _combine_body,
        out_shape=jax.ShapeDtypeStruct((t, d), F32),
        grid=(n_tiles,),
        in_specs=_tile_index_specs(n_tiles, rows) + _tile_index_specs(n_tiles, rows) + [
            pl.BlockSpec(memory_space=pl.ANY),
            pl.BlockSpec((rows, d), lambda i: (i, 0)),
            pl.BlockSpec((rows, LANES), lambda i: (i, 0)),
            pl.BlockSpec((1, d), lambda i: (0, 0))],
        out_specs=pl.BlockSpec((rows, d), lambda i: (i, 0)),
        scratch_shapes=[pltpu.VMEM((2, rows, d), F32), pltpu.VMEM((2, rows, d), F32),
                        pltpu.SemaphoreType.DMA((2,)), pltpu.SemaphoreType.DMA((2,))],
        compiler_params=_cparams(("arbitrary",)),
        name="moe_combine",
    )(d0, d0, d1, d1, y, h, info, g.reshape(1, d))


def _moe_final(h, ln_g, router_w, w1, w3, w2, final_g, tr=512, tn_up=512, tn_down=1024):
    t, d = h.shape
    n_exp = router_w.shape[1]
    tr = min(tr, t)
    info, cnt = _router(h, ln_g, router_w)

    expert = info[:, 0:TOP_K].astype(jnp.int32)
    rank = info[:, 4:4 + TOP_K].astype(jnp.int32)
    counts = cnt[0, :n_exp].astype(jnp.int32)
    tiles_per = (counts + tr - 1) // tr
    tile_end = jnp.cumsum(tiles_per)
    tile_start = tile_end - tiles_per
    n_tiles = TOP_K * t // tr + n_exp
    total = n_tiles * tr
    dest = tile_start[expert] * tr + rank
    token = jnp.broadcast_to(jnp.arange(t, dtype=jnp.int32)[:, None], (t, TOP_K))
    row_token = jnp.zeros((total,), jnp.int32).at[dest.reshape(-1)].set(token.reshape(-1))
    tile_id = jnp.arange(n_tiles, dtype=jnp.int32)
    n_valid = tile_end[n_exp - 1]
    last = jnp.maximum(n_valid - 1, 0)
    tile_valid = (tile_id < n_valid).astype(jnp.int32)
    tile_src = jnp.minimum(tile_id, last)
    ended = (tile_end[None, :] <= tile_src[:, None]).astype(jnp.int32)
    tile_expert = jnp.minimum(jnp.sum(ended, axis=1), n_exp - 1)
    tables = (tile_expert, tile_valid, tile_src)

    xs = _gather_norm(h, ln_g, row_token)
    gs = _grouped(_gffn1_body, "moe_up", xs, (w1, w3), tables, tr, tn_up, BF16)
    ys = _grouped(_gmm_body, "moe_down", gs, (w2,), tables, tr, tn_down, F32)
    return _combine_norm(h, ys, dest, info, final_g)


def kernel(x, ln1_g, w_in, b_f, w_s, b_s, conv_w, lam_q1, lam_k1, lam_q2, lam_k2, subln_g, w_o, ln2_g,
           ffn_w1, ffn_w3, ffn_w2, router_w, moe_w1, moe_w3, moe_w2, final_g):
    bsz, seq, d = x.shape
    depth = ln1_g.shape[0]
    nh = b_f.shape[1]
    dg = nh * HEAD_DIM
    t = bsz * seq
    assert w_in.shape[2] == 11 * dg + nh and d == 4 * dg and depth % 2 == 0
    f0 = 5 * dg
    blk = lambda cols: cols // HEAD_DIM
    col_fq, col_fk, col_fv = blk(2 * dg), blk(3 * dg), blk(4 * dg)
    col_dq, col_dk, col_dv = blk(3 * dg), blk(4 * dg), blk(5 * dg)

    h = x.reshape(t, d)
    out = None
    for i in range(depth):
        w_b = w_in[i, :, f0 + nh:][None]
        w_f = jnp.zeros((d, LANES), BF16).at[:, :nh].set(w_in[i, :, f0:f0 + nh].astype(BF16))

        n, flog = _ln1(h, ln1_g[i], w_f)
        proj_a = _matmul(n, w_in, i, f0, BF16, tm=1024, tn=512)
        proj_b = _matmul(n, w_b, 0, 6 * dg, BF16, tm=1024, tn=512)
        proj_a3 = proj_a.reshape(bsz, seq, f0)
        proj_b3 = proj_b.reshape(bsz, seq, 6 * dg)

        out_a = _sgu(proj_a, w_s[i], b_s[i], dg)
        c_col = _logsig_cumsum(flog.reshape(bsz, seq, LANES), b_f[i])
        c_row = jnp.transpose(c_col[:, :, :nh], (0, 2, 1))
        out_b = _fox(proj_a3, c_col, c_row, nh, col_fq, col_fk, col_fv).reshape(t, dg)
        out_c = _conv(proj_b, conv_w[i], seq, dg, 0)
        lam_init = 0.8 - 0.6 * math.exp(-0.3 * i)
        lam_vecs = jnp.stack([lam_q1[i], lam_k1[i], lam_q2[i], lam_k2[i]]).astype(F32)
        out_d = _diff(proj_b3, lam_vecs, subln_g[i], nh, col_dq, col_dk, col_dv, lam_init).reshape(t, dg)

        h = _out_proj((out_a, out_b, out_c, out_d), w_o, i, h, tm=1024, tn=512)

        j = i // 2
        if i % 2 == 0:
            n2 = _rmsnorm(h, ln2_g[i], BF16)
            g = _ffn1(n2, ffn_w1[j], ffn_w3[j], tm=1024, tn=256)
            h = _ffn2(g, ffn_w2[j], h, tm=1024, tn=256)
        else:
            assert i == depth - 1
            out = _moe_final(h, ln2_g[i], router_w[j], moe_w1[j], moe_w3[j], moe_w2[j], final_g)
    return out.reshape(bsz, seq, d)
```

```python
import functools
import math

import jax
import jax.numpy as jnp
from jax import lax
from jax.experimental import pallas as pl
from jax.experimental.pallas import tpu as pltpu

F32 = jnp.float32
BF16 = jnp.bfloat16

HEAD_DIM = 128
CHUNK = 64
SGU_BLOCK = 128
CONV_WIDTH = 3
TOP_K = 2
RMS_EPS = 1e-6
NEG_INF = -1e30
LOG2E = 1.4426950408889634
LANES = 128
BF16_ROWS = 16
ROW_DMA_UNROLL = 8
VMEM_LIMIT = 58 * 1024 * 1024


def _cparams(sem, vmem=None):
    return pltpu.CompilerParams(dimension_semantics=sem, vmem_limit_bytes=vmem)


def _tile(dim, target, unit=LANES):
    if dim <= target:
        return dim
    best = max(c for c in range(unit, target + 1, unit) if dim % c == 0)
    return best


def _rms(x, g):
    ms = jnp.mean(x * x, axis=-1, keepdims=True)
    return x * lax.rsqrt(ms + RMS_EPS) * g


def _ln1_body(x_ref, g_ref, wf_ref, n_ref, f_ref):
    nb = _rms(x_ref[...], g_ref[...]).astype(BF16)
    n_ref[...] = nb
    f_ref[...] = jnp.dot(nb, wf_ref[...], preferred_element_type=F32)


def _ln1(h, g, wf, tm=256):
    t, d = h.shape
    tm = min(tm, t)
    return pl.pallas_call(
        _ln1_body,
        out_shape=(jax.ShapeDtypeStruct((t, d), BF16), jax.ShapeDtypeStruct((t, LANES), F32)),
        grid=(t // tm,),
        in_specs=[pl.BlockSpec((tm, d), lambda i: (i, 0)),
                  pl.BlockSpec((1, d), lambda i: (0, 0)),
                  pl.BlockSpec((d, LANES), lambda i: (0, 0))],
        out_specs=(pl.BlockSpec((tm, d), lambda i: (i, 0)),
                   pl.BlockSpec((tm, LANES), lambda i: (i, 0))),
        compiler_params=_cparams(("parallel",)),
        name="ln1_forget",
    )(h, g.reshape(1, d), wf)


def _rmsnorm_body(x_ref, g_ref, o_ref):
    o_ref[...] = _rms(x_ref[...], g_ref[...]).astype(o_ref.dtype)


def _rmsnorm(h, g, out_dtype, tm=256):
    t, d = h.shape
    tm = min(tm, t)
    return pl.pallas_call(
        _rmsnorm_body,
        out_shape=jax.ShapeDtypeStruct((t, d), out_dtype),
        grid=(t // tm,),
        in_specs=[pl.BlockSpec((tm, d), lambda i: (i, 0)),
                  pl.BlockSpec((1, d), lambda i: (0, 0))],
        out_specs=pl.BlockSpec((tm, d), lambda i: (i, 0)),
        compiler_params=_cparams(("parallel",)),
        name="rmsnorm",
    )(h, g.reshape(1, d))


def _mm_body(x_ref, w_ref, o_ref):
    o_ref[...] = jnp.dot(x_ref[...], w_ref[...].astype(BF16),
                         preferred_element_type=F32).astype(o_ref.dtype)


def _matmul(x, w, out_dtype, tm, tn):
    t, k = x.shape
    n = w.shape[1]
    tm, tn = _tile(t, tm, 8), _tile(n, tn)
    return pl.pallas_call(
        _mm_body,
        out_shape=jax.ShapeDtypeStruct((t, n), out_dtype),
        grid=(t // tm, n // tn),
        in_specs=[pl.BlockSpec((tm, k), lambda i, j: (i, 0)),
                  pl.BlockSpec((k, tn), lambda i, j: (0, j))],
        out_specs=pl.BlockSpec((tm, tn), lambda i, j: (i, j)),
        compiler_params=_cparams(("parallel", "arbitrary"), VMEM_LIMIT),
        name="in_proj",
    )(x, w)


def _oproj_body(a_ref, b_ref, c_ref, d_ref, w_ref, r_ref, o_ref, *, dg):
    acc = r_ref[...]
    for part, x_ref in enumerate((a_ref, b_ref, c_ref, d_ref)):
        w = w_ref[part * dg:(part + 1) * dg, :].astype(BF16)
        acc = acc + jnp.dot(x_ref[...], w, preferred_element_type=F32)
    o_ref[...] = acc


def _out_proj(parts, w, layer, res, tm, tn):
    t, dg = parts[0].shape
    _, k, n = w.shape
    tm, tn = _tile(t, tm, 8), _tile(n, tn)
    xspec = pl.BlockSpec((tm, dg), lambda i, j: (i, 0))
    return pl.pallas_call(
        functools.partial(_oproj_body, dg=dg),
        out_shape=jax.ShapeDtypeStruct((t, n), F32),
        grid=(t // tm, n // tn),
        in_specs=[xspec, xspec, xspec, xspec,
                  pl.BlockSpec((None, k, tn), lambda i, j: (layer, 0, j)),
                  pl.BlockSpec((tm, tn), lambda i, j: (i, j))],
        out_specs=pl.BlockSpec((tm, tn), lambda i, j: (i, j)),
        compiler_params=_cparams(("parallel", "arbitrary"), VMEM_LIMIT),
        name="out_proj",
    )(*parts, w, res)


def _silu(x):
    return x * (1.0 / (1.0 + jnp.exp(-x)))


def _ffn1_body(x_ref, w1_ref, w3_ref, o_ref):
    x = x_ref[...]
    a = jnp.dot(x, w1_ref[...].astype(BF16), preferred_element_type=F32)
    b = jnp.dot(x, w3_ref[...].astype(BF16), preferred_element_type=F32)
    o_ref[...] = (_silu(a) * b).astype(o_ref.dtype)


def _ffn1(x, w1, w3, tm, tn):
    t, k = x.shape
    n = w1.shape[1]
    tm, tn = _tile(t, tm, 8), _tile(n, tn)
    wspec = pl.BlockSpec((k, tn), lambda i, j: (0, j))
    return pl.pallas_call(
        _ffn1_body,
        out_shape=jax.ShapeDtypeStruct((t, n), BF16),
        grid=(t // tm, n // tn),
        in_specs=[pl.BlockSpec((tm, k), lambda i, j: (i, 0)), wspec, wspec],
        out_specs=pl.BlockSpec((tm, tn), lambda i, j: (i, j)),
        compiler_params=_cparams(("parallel", "arbitrary"), VMEM_LIMIT),
        name="ffn_up",
    )(x, w1, w3)


def _mm_res_body(x_ref, w_ref, r_ref, o_ref):
    o_ref[...] = r_ref[...] + jnp.dot(x_ref[...], w_ref[...].astype(BF16), preferred_element_type=F32)


def _ffn2(x, w, res, tm, tn):
    t, k = x.shape
    n = w.shape[1]
    tm, tn = _tile(t, tm, 8), _tile(n, tn)
    return pl.pallas_call(
        _mm_res_body,
        out_shape=jax.ShapeDtypeStruct((t, n), F32),
        grid=(t // tm, n // tn),
        in_specs=[pl.BlockSpec((tm, k), lambda i, j: (i, 0), pipeline_mode=pl.Buffered(1)),
                  pl.BlockSpec((k, tn), lambda i, j: (0, j)),
                  pl.BlockSpec((tm, tn), lambda i, j: (i, j))],
        out_specs=pl.BlockSpec((tm, tn), lambda i, j: (i, j)),
        compiler_params=_cparams(("parallel", "arbitrary"), VMEM_LIMIT),
        name="ffn_down",
    )(x, w, res)


def _gelu_tanh(x):
    return 0.5 * x * (1.0 + jnp.tanh(math.sqrt(2.0 / math.pi) * (x + 0.044715 * (x * x * x))))


def _sgu_body(z_ref, w_ref, b_ref, o_ref, *, dg, nh):
    rows = z_ref.shape[0]
    z = _gelu_tanh(z_ref[...].astype(F32))
    u = z[:, :dg]
    v = z[:, dg:]
    mu = jnp.mean(v, axis=-1, keepdims=True)
    vc = v - mu
    var = jnp.mean(vc * vc, axis=-1, keepdims=True)
    vn = (vc * lax.rsqrt(var + RMS_EPS)).astype(BF16)
    t_chunk = lax.broadcasted_iota(jnp.int32, (SGU_BLOCK, SGU_BLOCK), 0) // CHUNK
    s_chunk = lax.broadcasted_iota(jnp.int32, (SGU_BLOCK, SGU_BLOCK), 1) // CHUNK
    allowed = t_chunk >= s_chunk
    for hd in range(nh):
        wm = jnp.where(allowed, w_ref[hd], 0.0).astype(BF16)
        bias = b_ref[hd]
        cs = slice(hd * HEAD_DIM, (hd + 1) * HEAD_DIM)
        for blk in range(rows // SGU_BLOCK):
            rs = slice(blk * SGU_BLOCK, (blk + 1) * SGU_BLOCK)
            mixed = jnp.dot(wm, vn[rs, cs], preferred_element_type=F32) + bias
            o_ref[rs, cs] = (u[rs, cs] * mixed).astype(o_ref.dtype)


def _sgu(proj, w_s, b_s, dg, rows=256):
    t = proj.shape[0]
    nh = w_s.shape[0]
    rows = min(rows, t)
    bias = jnp.broadcast_to(b_s[:, :, None], (nh, SGU_BLOCK, HEAD_DIM))
    return pl.pallas_call(
        functools.partial(_sgu_body, dg=dg, nh=nh),
        out_shape=jax.ShapeDtypeStruct((t, dg), BF16),
        grid=(t // rows,),
        in_specs=[pl.BlockSpec((rows, 2 * dg), lambda i: (i, 0)),
                  pl.BlockSpec((nh, SGU_BLOCK, SGU_BLOCK), lambda i: (0, 0, 0)),
                  pl.BlockSpec((nh, SGU_BLOCK, HEAD_DIM), lambda i: (0, 0, 0))],
        out_specs=pl.BlockSpec((rows, dg), lambda i: (i, 0)),
        compiler_params=_cparams(("parallel",)),
        name="sgu",
    )(proj, w_s, bias)


def _conv_body(x_ref, b_ref, c_ref, xh_ref, ch_ref, w_ref, o_ref, *, tiles_per_seq):
    i = pl.program_id(0)
    rows = x_ref.shape[0]
    hcur = c_ref[...].astype(F32) * x_ref[...].astype(F32)
    halo = ch_ref[...].astype(F32) * xh_ref[...].astype(F32)
    halo = jnp.where(i % tiles_per_seq == 0, 0.0, halo)
    hp = jnp.concatenate([halo, hcur], axis=0)
    w = w_ref[...]
    y = w[CONV_WIDTH - 1:CONV_WIDTH, :] * hcur
    for tap in range(CONV_WIDTH - 1):
        off = BF16_ROWS - (CONV_WIDTH - 1) + tap
        y = y + w[tap:tap + 1, :] * hp[off:off + rows, :]
    o_ref[...] = (b_ref[...].astype(F32) * y).astype(o_ref.dtype)


def _conv(proj, conv_w, seq, dg, col0, rows=512):
    t = proj.shape[0]
    rows = min(rows, seq)
    cb = col0 // dg
    hb = rows // BF16_ROWS
    cur = lambda off: pl.BlockSpec((rows, dg), lambda i: (i, cb + off))
    halo = lambda off: pl.BlockSpec((BF16_ROWS, dg), lambda i: (jnp.maximum(i * hb - 1, 0), cb + off))
    wpad = jnp.zeros((8, dg), F32).at[:CONV_WIDTH].set(conv_w)
    return pl.pallas_call(
        functools.partial(_conv_body, tiles_per_seq=seq // rows),
        out_shape=jax.ShapeDtypeStruct((t, dg), BF16),
        grid=(t // rows,),
        in_specs=[cur(0), cur(1), cur(2), halo(0), halo(2),
                  pl.BlockSpec((8, dg), lambda i: (0, 0))],
        out_specs=pl.BlockSpec((rows, dg), lambda i: (i, 0)),
        compiler_params=_cparams(("parallel",)),
        name="short_conv",
    )(proj, proj, proj, proj, proj, wpad)


def _logsig_cumsum_body(f_ref, b_ref, c_ref, carry_ref):
    s = pl.program_id(1)

    @pl.when(s == 0)
    def _():
        carry_ref[...] = jnp.zeros_like(carry_ref)

    x = f_ref[0] + b_ref[...]
    ls = jnp.minimum(x, 0.0) - jnp.log(1.0 + jnp.exp(-jnp.abs(x)))
    rows = ls.shape[0]
    row = lax.broadcasted_iota(jnp.int32, ls.shape, 0)
    shift = 1
    while shift < rows:
        ls = ls + jnp.where(row >= shift, pltpu.roll(ls, shift, 0), 0.0)
        shift *= 2
    out = ls + carry_ref[0:1, :]
    c_ref[0] = out
    carry_ref[...] = jnp.broadcast_to(out[rows - 1:rows, :], carry_ref.shape)


def _logsig_cumsum(f, b_f, rows=512):
    bsz, seq, _ = f.shape
    rows = min(rows, seq)
    bias = jnp.zeros((1, LANES), F32).at[0, :b_f.shape[0]].set(b_f)
    return pl.pallas_call(
        _logsig_cumsum_body,
        out_shape=jax.ShapeDtypeStruct(f.shape, F32),
        grid=(bsz, seq // rows),
        in_specs=[pl.BlockSpec((1, rows, LANES), lambda b, s: (b, s, 0)),
                  pl.BlockSpec((1, LANES), lambda b, s: (0, 0))],
        out_specs=pl.BlockSpec((1, rows, LANES), lambda b, s: (b, s, 0)),
        scratch_shapes=[pltpu.VMEM((8, LANES), F32)],
        compiler_params=_cparams(("parallel", "arbitrary")),
        name="forget_cumsum",
    )(f, bias)


def _qk(q, k):
    return lax.dot_general(q, k, (((1,), (1,)), ((), ())), preferred_element_type=F32)


def _online_softmax(t, row_term, v, carry):
    m, l, acc = carry
    m_new = jnp.maximum(m, jnp.max(t, axis=1, keepdims=True) + row_term)
    p = jnp.exp2(t - (m_new - row_term))
    alpha = jnp.exp2(m - m_new)
    l = alpha * l + jnp.sum(p, axis=1, keepdims=True)
    acc = alpha * acc + jnp.dot(p.astype(BF16), v, preferred_element_type=F32)
    return m_new, l, acc


def _below_diagonal(step, n_blocks, carry):
    def pair(p, c):
        return step(2 * p + 1, step(2 * p, c, False), False)

    carry = lax.fori_loop(0, n_blocks // 2, pair, carry)
    return lax.fori_loop(2 * (n_blocks // 2), n_blocks, lambda j, c: step(j, c, False), carry)


def _softmax_init(rows):
    return (jnp.full((rows, 1), NEG_INF, F32), jnp.zeros((rows, 1), F32), jnp.zeros((rows, HEAD_DIM), F32))


def _fox_body(q_ref, k_ref, v_ref, ccol_ref, crow_ref, o_ref, *, tq, scale):
    hd = pl.program_id(1)
    qi = pl.program_id(2)
    q = q_ref[0]
    lane = lax.broadcasted_iota(jnp.int32, (tq, LANES), 1)
    cq = jnp.sum(jnp.where(lane == hd, ccol_ref[0], 0.0), axis=1, keepdims=True) * LOG2E

    def step(j, carry, diagonal):
        start = pl.multiple_of(j * tq, tq)
        k = k_ref[0, pl.ds(start, tq), :]
        v = v_ref[0, pl.ds(start, tq), :]
        t = _qk(q, k) * (scale * LOG2E) - crow_ref[0, 0, pl.ds(j, 1), :] * LOG2E
        if diagonal:
            r = lax.broadcasted_iota(jnp.int32, (tq, tq), 0)
            c = lax.broadcasted_iota(jnp.int32, (tq, tq), 1)
            t = jnp.where(c <= r, t, NEG_INF)
        return _online_softmax(t, cq, v, carry)

    _, l, acc = step(qi, _below_diagonal(step, qi, _softmax_init(tq)), True)
    o_ref[0] = (acc / l).astype(o_ref.dtype)


def _fox(proj3, c_col, c_row, nh, col_q, col_k, col_v, tq=512):
    bsz, seq, _ = proj3.shape
    tq = min(tq, seq)
    nq = seq // tq
    c_row = c_row.reshape(bsz, nh, nq, tq)
    kv = lambda col: pl.BlockSpec((1, seq, HEAD_DIM), lambda b, h, i: (b, 0, col + h))
    return pl.pallas_call(
        functools.partial(_fox_body, tq=tq, scale=HEAD_DIM ** -0.5),
        out_shape=jax.ShapeDtypeStruct((bsz, seq, nh * HEAD_DIM), BF16),
        grid=(bsz, nh, nq),
        in_specs=[pl.BlockSpec((1, tq, HEAD_DIM), lambda b, h, i: (b, i, col_q + h)),
                  kv(col_k), kv(col_v),
                  pl.BlockSpec((1, tq, LANES), lambda b, h, i: (b, i, 0)),
                  pl.BlockSpec((1, 1, nq, tq), lambda b, h, i: (b, h, 0, 0))],
        out_specs=pl.BlockSpec((1, tq, HEAD_DIM), lambda b, h, i: (b, i, h)),
        compiler_params=_cparams(("parallel", "parallel", "arbitrary")),
        name="fox_attention",
    )(proj3, proj3, proj3, c_col, c_row)


def _diff_body(slope_ref, q_ref, k_ref, v_ref, lam_ref, g_ref, o_ref, *, tq, scale, lam_init):
    hd = pl.program_id(1)
    qi = pl.program_id(2)
    half = HEAD_DIM // 2
    slope = slope_ref[hd] * LOG2E
    lv = lam_ref[...]
    lam = (jnp.exp(jnp.sum(lv[0:1] * lv[1:2], axis=1, keepdims=True))
           - jnp.exp(jnp.sum(lv[2:3] * lv[3:4], axis=1, keepdims=True)) + lam_init)
    q = q_ref[0]
    first = lax.broadcasted_iota(jnp.int32, (tq, HEAD_DIM), 1) < half
    zero = jnp.zeros_like(q)
    q2 = jnp.concatenate([jnp.where(first, q, zero), jnp.where(first, zero, q)], axis=0)
    row = lax.broadcasted_iota(jnp.int32, (2 * tq, 1), 0)
    row = jnp.where(row >= tq, row - tq, row)
    col_bias = slope * lax.broadcasted_iota(jnp.int32, (1, tq), 1).astype(F32)

    def step(j, carry, diagonal):
        start = pl.multiple_of(j * tq, tq)
        k = k_ref[0, pl.ds(start, tq), :]
        v = v_ref[0, pl.ds(start, tq), :]
        t = _qk(q2, k) * (scale * LOG2E)
        if diagonal:
            c = lax.broadcasted_iota(jnp.int32, (2 * tq, tq), 1)
            t = jnp.where((c // CHUNK) <= (row // CHUNK), t - slope * jnp.abs(row - c).astype(F32), NEG_INF)
            row_term = jnp.zeros((2 * tq, 1), F32)
        else:
            t = t + col_bias
            row_term = -slope * (row + (qi - j) * tq).astype(F32)
        return _online_softmax(t, row_term, v, carry)

    _, l, acc = step(qi, _below_diagonal(step, qi, _softmax_init(2 * tq)), True)
    a = acc / l
    out = a[:tq] - lam * a[tq:]
    o_ref[0] = (_rms(out, g_ref[...]) * (1.0 - lam_init)).astype(o_ref.dtype)


def _diff(proj3, lam_vecs, subln_g, nh, col_q, col_k, col_v, lam_init, tq=512):
    bsz, seq, _ = proj3.shape
    tq = min(tq, seq)
    nq = seq // tq
    slopes = (2.0 ** (-8.0 * jnp.arange(1, nh + 1, dtype=F32) / nh)).astype(F32)
    kv = lambda col: pl.BlockSpec((1, seq, HEAD_DIM), lambda b, h, i, s: (b, 0, col + h))
    return pl.pallas_call(
        functools.partial(_diff_body, tq=tq, scale=(HEAD_DIM // 2) ** -0.5, lam_init=lam_init),
        out_shape=jax.ShapeDtypeStruct((bsz, seq, nh * HEAD_DIM), BF16),
        grid_spec=pltpu.PrefetchScalarGridSpec(
            num_scalar_prefetch=1,
            grid=(bsz, nh, nq),
            in_specs=[pl.BlockSpec((1, tq, HEAD_DIM), lambda b, h, i, s: (b, i, col_q + h)),
                      kv(col_k), kv(col_v),
                      pl.BlockSpec(lam_vecs.shape, lambda b, h, i, s: (0, 0)),
                      pl.BlockSpec((1, HEAD_DIM), lambda b, h, i, s: (0, 0))],
            out_specs=pl.BlockSpec((1, tq, HEAD_DIM), lambda b, h, i, s: (b, i, h)),
        ),
        compiler_params=_cparams(("parallel", "parallel", "arbitrary")),
        name="diff_attention",
    )(slopes, proj3, proj3, proj3, lam_vecs, subln_g.reshape(1, HEAD_DIM))


def _router_body(h_ref, g_ref, rw_ref, info_ref, cnt_ref, carry_ref, *, n_exp):
    i = pl.program_id(0)

    @pl.when(i == 0)
    def _():
        carry_ref[...] = jnp.zeros_like(carry_ref)

    tm = h_ref.shape[0]
    n = _rms(h_ref[...], g_ref[...])
    logits = jnp.dot(n, rw_ref[...], preferred_element_type=F32, precision=lax.Precision.HIGHEST)
    lane = lax.broadcasted_iota(jnp.int32, (tm, LANES), 1).astype(F32)
    logits = jnp.where(lane < n_exp, logits, -jnp.inf)
    v1 = jnp.max(logits, axis=1, keepdims=True)
    i1 = jnp.min(jnp.where(logits == v1, lane, float(LANES)), axis=1, keepdims=True)
    rest = jnp.where(lane == i1, -jnp.inf, logits)
    v2 = jnp.max(rest, axis=1, keepdims=True)
    i2 = jnp.min(jnp.where(rest == v2, lane, float(LANES)), axis=1, keepdims=True)
    ratio = jnp.exp(v2 - v1)
    g1 = 1.0 / (1.0 + ratio)
    g2 = ratio / (1.0 + ratio)
    hit1 = lane == i1
    hit2 = lane == i2
    onehot = jnp.where(hit1 | hit2, 1.0, 0.0)
    r = lax.broadcasted_iota(jnp.int32, (tm, tm), 0)
    c = lax.broadcasted_iota(jnp.int32, (tm, tm), 1)
    before = jnp.where(c < r, 1.0, 0.0).astype(BF16)
    rank = jnp.dot(before, onehot.astype(BF16), preferred_element_type=F32) + carry_ref[0:1, :]
    rank1 = jnp.sum(jnp.where(hit1, rank, 0.0), axis=1, keepdims=True)
    rank2 = jnp.sum(jnp.where(hit2, rank, 0.0), axis=1, keepdims=True)
    total = carry_ref[0:1, :] + jnp.sum(onehot, axis=0, keepdims=True)
    carry_ref[...] = jnp.broadcast_to(total, carry_ref.shape)
    cnt_ref[...] = jnp.broadcast_to(total, cnt_ref.shape)
    fields = (i1, i2, g1, g2, rank1, rank2)
    info = jnp.zeros((tm, LANES), F32)
    for pos, val in enumerate(fields):
        info = jnp.where(lane == pos, val, info)
    info_ref[...] = info


def _router(h, g, router_w, tm=256):
    t, d = h.shape
    n_exp = router_w.shape[1]
    tm = min(tm, t)
    rw = jnp.zeros((d, LANES), F32).at[:, :n_exp].set(router_w)
    return pl.pallas_call(
        functools.partial(_router_body, n_exp=n_exp),
        out_shape=(jax.ShapeDtypeStruct((t, LANES), F32), jax.ShapeDtypeStruct((8, LANES), F32)),
        grid=(t // tm,),
        in_specs=[pl.BlockSpec((tm, d), lambda i: (i, 0)),
                  pl.BlockSpec((1, d), lambda i: (0, 0)),
                  pl.BlockSpec((d, LANES), lambda i: (0, 0))],
        out_specs=(pl.BlockSpec((tm, LANES), lambda i: (i, 0)),
                   pl.BlockSpec((8, LANES), lambda i: (0, 0))),
        scratch_shapes=[pltpu.VMEM((8, LANES), F32)],
        compiler_params=_cparams(("arbitrary",)),
        name="router",
    )(h, g.reshape(1, d), rw)


def _row_copy(src_hbm, dst, sem, src_row, dst_row):
    return pltpu.make_async_copy(src_hbm.at[pl.ds(src_row, 1)], dst.at[pl.ds(dst_row, 1)], sem)


def _start_rows(src_hbm, idx_ref, dst, sem):
    def body(g, c):
        for u in range(ROW_DMA_UNROLL):
            r = g * ROW_DMA_UNROLL + u
            _row_copy(src_hbm, dst, sem, idx_ref[0, 0, r], r).start(priority=u % 2)
        return c

    lax.fori_loop(0, dst.shape[0] // ROW_DMA_UNROLL, body, 0)


def _wait_rows(src_hbm, dst, sem):
    def body(r, c):
        _row_copy(src_hbm, dst, sem, 0, r).wait()
        return c

    lax.fori_loop(0, dst.shape[0], body, 0, unroll=ROW_DMA_UNROLL)


def _prefetch_schedule(issue):
    i = pl.program_id(0)
    slot = i % 2

    @pl.when(i == 0)
    def _():
        issue(False, 0)

    @pl.when(i + 1 < pl.num_programs(0))
    def _():
        issue(True, 1 - slot)

    return slot


def _gather_norm_body(idx_ref, nxt_ref, h_hbm, g_ref, o_ref, buf, sem):
    def issue(ahead, slot):
        _start_rows(h_hbm, nxt_ref if ahead else idx_ref, buf.at[slot], sem.at[slot])

    slot = _prefetch_schedule(issue)
    _wait_rows(h_hbm, buf.at[slot], sem.at[slot])
    o_ref[...] = _rms(buf[slot], g_ref[...]).astype(o_ref.dtype)


def _tile_index_specs(n_tiles, rows):
    return [pl.BlockSpec((1, 1, rows), lambda i: (i, 0, 0), memory_space=pltpu.SMEM),
            pl.BlockSpec((1, 1, rows), lambda i: (jnp.minimum(i + 1, n_tiles - 1), 0, 0), memory_space=pltpu.SMEM)]


def _gather_norm(h, g, row_token, rows=256):
    d = h.shape[1]
    total = row_token.shape[0]
    n_tiles = total // rows
    idx = row_token.reshape(n_tiles, 1, rows)
    return pl.pallas_call(
        _gather_norm_body,
        out_shape=jax.ShapeDtypeStruct((total, d), BF16),
        grid=(n_tiles,),
        in_specs=_tile_index_specs(n_tiles, rows) + [
            pl.BlockSpec(memory_space=pl.ANY),
            pl.BlockSpec((1, d), lambda i: (0, 0))],
        out_specs=pl.BlockSpec((rows, d), lambda i: (i, 0)),
        scratch_shapes=[pltpu.VMEM((2, rows, d), F32), pltpu.SemaphoreType.DMA((2,))],
        compiler_params=_cparams(("arbitrary",)),
        name="moe_gather",
    )(idx, idx, h, g.reshape(1, d))


def _group_first(te_ref, tv_ref, r):
    prev = te_ref[jnp.maximum(r - 1, 0)]
    return (r == 0) | (te_ref[r] != prev)


def _gffn1_body(te_ref, tv_ref, tx_ref, x_ref, w1_ref, w3_ref, o_ref, w1b, w3b):
    r = pl.program_id(1)

    @pl.when(_group_first(te_ref, tv_ref, r))
    def _():
        w1b[...] = w1_ref[0].astype(BF16)
        w3b[...] = w3_ref[0].astype(BF16)

    @pl.when(tv_ref[r] == 1)
    def _():
        x = x_ref[...]
        a = jnp.dot(x, w1b[...], preferred_element_type=F32)
        b = jnp.dot(x, w3b[...], preferred_element_type=F32)
        o_ref[...] = (_silu(a) * b).astype(o_ref.dtype)

    @pl.when(tv_ref[r] == 0)
    def _():
        o_ref[...] = jnp.zeros_like(o_ref)


def _gmm_body(te_ref, tv_ref, tx_ref, x_ref, w_ref, o_ref, wb):
    r = pl.program_id(1)

    @pl.when(_group_first(te_ref, tv_ref, r))
    def _():
        wb[...] = w_ref[0].astype(BF16)

    @pl.when(tv_ref[r] == 1)
    def _():
        o_ref[...] = jnp.dot(x_ref[...], wb[...], preferred_element_type=F32).astype(o_ref.dtype)

    @pl.when(tv_ref[r] == 0)
    def _():
        o_ref[...] = jnp.zeros_like(o_ref)


def _grouped(body, name, x, ws, tables, tr, tn, out_dtype):
    total, k = x.shape
    n = ws[0].shape[2]
    tn = _tile(n, tn)
    n_tiles = total // tr
    wspec = pl.BlockSpec((1, k, tn), lambda j, r, te, tv, tx: (te[r], 0, j))
    return pl.pallas_call(
        body,
        out_shape=jax.ShapeDtypeStruct((total, n), out_dtype),
        grid_spec=pltpu.PrefetchScalarGridSpec(
            num_scalar_prefetch=3,
            grid=(n // tn, n_tiles),
            in_specs=[pl.BlockSpec((tr, k), lambda j, r, te, tv, tx: (tx[r], 0))] + [wspec] * len(ws),
            out_specs=pl.BlockSpec((tr, tn), lambda j, r, te, tv, tx: (r, j)),
            scratch_shapes=[pltpu.VMEM((k, tn), BF16)] * len(ws),
        ),
        compiler_params=_cparams(("arbitrary", "arbitrary"), VMEM_LIMIT),
        name=name,
    )(*tables, x, *ws)


def _combine_body(d0_ref, n0_ref, d1_ref, n1_ref, y_hbm, h_ref, info_ref, g_ref, o_ref, buf0, buf1, sem0, sem1):
    def issue(ahead, slot):
        _start_rows(y_hbm, n0_ref if ahead else d0_ref, buf0.at[slot], sem0.at[slot])
        _start_rows(y_hbm, n1_ref if ahead else d1_ref, buf1.at[slot], sem1.at[slot])

    slot = _prefetch_schedule(issue)
    _wait_rows(y_hbm, buf0.at[slot], sem0.at[slot])
    _wait_rows(y_hbm, buf1.at[slot], sem1.at[slot])
    info = info_ref[...]
    lane = lax.broadcasted_iota(jnp.int32, info.shape, 1)
    g0 = jnp.sum(jnp.where(lane == 2, info, 0.0), axis=1, keepdims=True)
    g1 = jnp.sum(jnp.where(lane == 3, info, 0.0), axis=1, keepdims=True)
    out = h_ref[...] + g0 * buf0[slot] + g1 * buf1[slot]
    o_ref[...] = _rms(out, g_ref[...])


def _combine_norm(h, y, dest, info, g, rows=256):
    t, d = h.shape
    rows = min(rows, t)
    n_tiles = t // rows
    d0 = dest[:, 0].reshape(n_tiles, 1, rows)
    d1 = dest[:, 1].reshape(n_tiles, 1, rows)
    return pl.pallas_call(
        _combine_body,
        out_shape=jax.ShapeDtypeStruct((t, d), F32),
        grid=(n_tiles,),
        in_specs=_tile_index_specs(n_tiles, rows) + _tile_index_specs(n_tiles, rows) + [
            pl.BlockSpec(memory_space=pl.ANY),
            pl.BlockSpec((rows, d), lambda i: (i, 0)),
            pl.BlockSpec((rows, LANES), lambda i: (i, 0)),
            pl.BlockSpec((1, d), lambda i: (0, 0))],
        out_specs=pl.BlockSpec((rows, d), lambda i: (i, 0)),
        scratch_shapes=[pltpu.VMEM((2, rows, d), F32), pltpu.VMEM((2, rows, d), F32),
                        pltpu.SemaphoreType.DMA((2,)), pltpu.SemaphoreType.DMA((2,))],
        compiler_params=_cparams(("arbitrary",)),
        name="moe_combine",
    )(d0, d0, d1, d1, y, h, info, g.reshape(1, d))


def _moe_final(h, ln_g, router_w, w1, w3, w2, final_g, tr=512, tn_up=512, tn_down=1024):
    t, d = h.shape
    n_exp = router_w.shape[1]
    tr = min(tr, t)
    info, cnt = _router(h, ln_g, router_w)

    expert = info[:, 0:TOP_K].astype(jnp.int32)
    rank = info[:, 4:4 + TOP_K].astype(jnp.int32)
    counts = cnt[0, :n_exp].astype(jnp.int32)
    tiles_per = (counts + tr - 1) // tr
    tile_end = jnp.cumsum(tiles_per)
    tile_start = tile_end - tiles_per
    n_tiles = TOP_K * t // tr + n_exp
    total = n_tiles * tr
    dest = tile_start[expert] * tr + rank
    token = jnp.broadcast_to(jnp.arange(t, dtype=jnp.int32)[:, None], (t, TOP_K))
    row_token = jnp.zeros((total,), jnp.int32).at[dest.reshape(-1)].set(token.reshape(-1))
    tile_id = jnp.arange(n_tiles, dtype=jnp.int32)
    n_valid = tile_end[n_exp - 1]
    last = jnp.maximum(n_valid - 1, 0)
    tile_valid = (tile_id < n_valid).astype(jnp.int32)
    tile_src = jnp.minimum(tile_id, last)
    tile_expert = jnp.minimum(jnp.searchsorted(tile_end, tile_src, side="right"), n_exp - 1).astype(jnp.int32)
    tables = (tile_expert, tile_valid, tile_src)

    xs = _gather_norm(h, ln_g, row_token)
    gs = _grouped(_gffn1_body, "moe_up", xs, (w1, w3), tables, tr, tn_up, BF16)
    ys = _grouped(_gmm_body, "moe_down", gs, (w2,), tables, tr, tn_down, F32)
    return _combine_norm(h, ys, dest, info, final_g)


def kernel(x, ln1_g, w_in, b_f, w_s, b_s, conv_w, lam_q1, lam_k1, lam_q2, lam_k2, subln_g, w_o, ln2_g,
           ffn_w1, ffn_w3, ffn_w2, router_w, moe_w1, moe_w3, moe_w2, final_g):
    bsz, seq, d = x.shape
    depth = ln1_g.shape[0]
    nh = b_f.shape[1]
    dg = nh * HEAD_DIM
    t = bsz * seq
    assert w_in.shape[2] == 11 * dg + nh and d == 4 * dg and depth % 2 == 0
    f0 = 5 * dg
    blk = lambda cols: cols // HEAD_DIM
    col_fq, col_fk, col_fv = blk(2 * dg), blk(3 * dg), blk(4 * dg)
    col_dq, col_dk, col_dv = blk(8 * dg), blk(9 * dg), blk(10 * dg)

    h = x.reshape(t, d)
    out = None
    for i in range(depth):
        w_main = jnp.concatenate([w_in[i, :, :f0], w_in[i, :, f0 + nh:]], axis=1).astype(BF16)
        w_f = jnp.zeros((d, LANES), BF16).at[:, :nh].set(w_in[i, :, f0:f0 + nh].astype(BF16))

        n, flog = _ln1(h, ln1_g[i], w_f)
        proj = _matmul(n, w_main, BF16, tm=1024, tn=1024)
        proj3 = proj.reshape(bsz, seq, proj.shape[1])

        out_a = _sgu(proj, w_s[i], b_s[i], dg)
        c_col = _logsig_cumsum(flog.reshape(bsz, seq, LANES), b_f[i])
        c_row = jnp.transpose(c_col[:, :, :nh], (0, 2, 1))
        out_b = _fox(proj3, c_col, c_row, nh, col_fq, col_fk, col_fv).reshape(t, dg)
        out_c = _conv(proj, conv_w[i], seq, dg, 5 * dg)
        lam_init = 0.8 - 0.6 * math.exp(-0.3 * i)
        lam_vecs = jnp.stack([lam_q1[i], lam_k1[i], lam_q2[i], lam_k2[i]]).astype(F32)
        out_d = _diff(proj3, lam_vecs, subln_g[i], nh, col_dq, col_dk, col_dv, lam_init).reshape(t, dg)

        h = _out_proj((out_a, out_b, out_c, out_d), w_o, i, h, tm=1024, tn=512)

        j = i // 2
        if i % 2 == 0:
            n2 = _rmsnorm(h, ln2_g[i], BF16)
            g = _ffn1(n2, ffn_w1[j], ffn_w3[j], tm=1024, tn=256)
            h = _ffn2(g, ffn_w2[j], h, tm=1024, tn=256)
        else:
            assert i == depth - 1
            out = _moe_final(h, ln2_g[i], router_w[j], moe_w1[j], moe_w3[j], moe_w2[j], final_g)
    return out.reshape(bsz, seq, d)
```
